```python
import jax, jax.numpy as jnp
from jax import lax
import numpy as np

D_MODEL = 4096
BATCH = 1
SEQ = 16384
DEPTH = 2
DEC_BATCH = 16
DEC_SEQ = 16
PAST_LEN = 1024

CHUNK = 64
N_MIXERS = 2
N_HGRN_LAYERS = (DEPTH + N_MIXERS - 1) // N_MIXERS
N_CONV_LAYERS = DEPTH // N_MIXERS
HGRN_EXPAND = 128
HGRN_HEADS = D_MODEL // HGRN_EXPAND
HGRN_DK = HGRN_EXPAND
HGRN_DV = D_MODEL // HGRN_HEADS
HGRN_F_DIM = HGRN_HEADS * HGRN_DK
CONV_WIDTH = 3
D_FF = ((8 * D_MODEL + 3 * 256 - 1) // (3 * 256)) * 256
NORM_EPS = 1e-6

kernel_name = "hgrn2_shortconv_stream_step"


def _rmsnorm(x, gain):
    xf = x.astype(jnp.float32)
    y = xf * lax.rsqrt(jnp.mean(xf * xf, axis=-1, keepdims=True) + NORM_EPS)
    return (y * gain.astype(jnp.float32)).astype(x.dtype)


def _swiglu(h, w_in, w_out):
    gate, up = jnp.split(h @ w_in, 2, axis=-1)
    return (jax.nn.silu(gate) * up) @ w_out


def _hgrn2_chunk_step(S, inp):
    q, k, v, g = inp
    C = q.shape[2]
    b = jnp.cumsum(g, axis=2)
    o_inter = jnp.einsum('bhtk,bhkv->bhtv', q * jnp.exp(b), S)
    causal = jnp.tril(jnp.ones((C, C), dtype=bool))
    diff = b[:, :, :, None, :] - b[:, :, None, :, :]
    decay = jnp.exp(jnp.where(causal[None, None, :, :, None], diff, -jnp.inf))
    A = jnp.einsum('bhtk,bhsk,bhtsk->bhts', q, k, decay)
    o = o_inter + jnp.einsum('bhts,bhsv->bhtv', A, v)
    b_last = b[:, :, -1:, :]
    S_new = jnp.exp(b_last[:, :, 0, :])[..., None] * S + jnp.einsum(
        'bhsk,bhsv->bhkv', k * jnp.exp(b_last - b), v)
    return S_new, o


def _hgrn2_mixer(h, S0, lb, w_in, out_gain, w_out, chunk):
    Bn, L, _ = h.shape
    n_chunks = L // chunk
    proj = h @ w_in
    zq, zf, zi, zg = jnp.split(proj, [HGRN_F_DIM, 2 * HGRN_F_DIM, 2 * HGRN_F_DIM + D_MODEL], axis=-1)
    q = jax.nn.silu(zq.astype(jnp.float32))
    zf32 = zf.astype(jnp.float32)
    f = lb + (1.0 - lb) * jax.nn.sigmoid(zf32)
    logf = jnp.log(f)
    k = (1.0 - lb) * jax.nn.sigmoid(-zf32)
    v = zi.astype(jnp.float32)

    def to_chunks(t, d):
        return t.reshape(Bn, n_chunks, chunk, HGRN_HEADS, d).transpose(1, 0, 3, 2, 4)

    S_fin, o = lax.scan(_hgrn2_chunk_step, S0.astype(jnp.float32),
                        (to_chunks(q, HGRN_DK), to_chunks(k, HGRN_DK),
                         to_chunks(v, HGRN_DV), to_chunks(logf, HGRN_DK)))
    o = o.transpose(1, 0, 3, 2, 4).reshape(Bn, L, HGRN_HEADS, HGRN_DV)
    o = o * lax.rsqrt(jnp.mean(o * o, axis=-1, keepdims=True) + NORM_EPS) * out_gain.astype(jnp.float32)
    o = o * jax.nn.silu(zg.astype(jnp.float32)).reshape(Bn, L, HGRN_HEADS, HGRN_DV)
    y = o.reshape(Bn, L, D_MODEL).astype(h.dtype) @ w_out
    return y, S_fin.astype(S0.dtype)


def _short_conv_mixer(h, buf, w_in, w_conv, w_out):
    gb, gc, u = jnp.split(h @ w_in, 3, axis=-1)
    cu = gc * u
    full = jnp.concatenate([buf.astype(cu.dtype), cu], axis=1)
    conv = lax.conv_general_dilated(full, w_conv.astype(cu.dtype)[:, None, :], window_strides=(1,),
                                    padding='VALID', dimension_numbers=('NWC', 'WIO', 'NWC'),
                                    feature_group_count=D_MODEL)
    y = (gb * conv) @ w_out
    return y, full[:, -(CONV_WIDTH - 1):, :].astype(buf.dtype)


def _trunk(x, hgrn_state, conv_state, chunk, lb_all, hgrn_w_in, hgrn_out_gain, hgrn_w_out,
           conv_w_in, conv_w, conv_w_out, norm_mix, norm_ffn, ffn_w_in, ffn_w_out, norm_final):
    new_h, new_c = [], []
    for i in range(DEPTH):
        h = _rmsnorm(x, norm_mix[i])
        j = i // N_MIXERS
        if i % N_MIXERS == 0:
            y, s = _hgrn2_mixer(h, hgrn_state[j], lb_all[i], hgrn_w_in[j], hgrn_out_gain[j],
                                hgrn_w_out[j], chunk)
            new_h.append(s)
        else:
            y, s = _short_conv_mixer(h, conv_state[j], conv_w_in[j], conv_w[j], conv_w_out[j])
            new_c.append(s)
        x = x + y
        x = x + _swiglu(_rmsnorm(x, norm_ffn[i]), ffn_w_in[i], ffn_w_out[i])
    return _rmsnorm(x, norm_final), jnp.stack(new_h), jnp.stack(new_c)


def setup_inputs(seed: int = 0) -> dict:
    key = jax.random.key(seed)
    ks = jax.random.split(key, 16)
    f32 = jnp.float32
    nrm = lambda k, shape, s: jax.random.normal(k, shape, f32) * s
    return {
        "x_prompt": nrm(ks[0], (BATCH, SEQ, D_MODEL), 1.0),
        "x_sample": nrm(ks[1], (DEC_BATCH, DEC_SEQ, D_MODEL), 1.0),
        "state_hgrn": nrm(ks[2], (N_HGRN_LAYERS, DEC_BATCH, HGRN_HEADS, HGRN_DK, HGRN_DV), 0.5),
        "state_conv": nrm(ks[3], (N_CONV_LAYERS, DEC_BATCH, CONV_WIDTH - 1, D_MODEL), 1.0),
        "hgrn_lb_logits": nrm(ks[4], (DEPTH + 1, HGRN_F_DIM), 0.1),
        "hgrn_w_in": nrm(ks[5], (N_HGRN_LAYERS, D_MODEL, 2 * HGRN_F_DIM + 2 * D_MODEL), D_MODEL ** -0.5),
        "hgrn_out_gain": 1.0 + nrm(ks[6], (N_HGRN_LAYERS, HGRN_DV), 0.02),
        "hgrn_w_out": nrm(ks[7], (N_HGRN_LAYERS, D_MODEL, D_MODEL), D_MODEL ** -0.5),
        "conv_w_in": nrm(ks[8], (N_CONV_LAYERS, D_MODEL, 3 * D_MODEL), D_MODEL ** -0.5),
        "conv_w": nrm(ks[9], (N_CONV_LAYERS, CONV_WIDTH, D_MODEL), CONV_WIDTH ** -0.5),
        "conv_w_out": nrm(ks[10], (N_CONV_LAYERS, D_MODEL, D_MODEL), D_MODEL ** -0.5),
        "norm_mix": 1.0 + nrm(ks[11], (DEPTH, D_MODEL), 0.02),
        "norm_ffn": 1.0 + nrm(ks[12], (DEPTH, D_MODEL), 0.02),
        "ffn_w_in": nrm(ks[13], (DEPTH, D_MODEL, 2 * D_FF), D_MODEL ** -0.5),
        "ffn_w_out": nrm(ks[14], (DEPTH, D_FF, D_MODEL), D_FF ** -0.5),
        "norm_final": 1.0 + nrm(ks[15], (D_MODEL,), 0.02),
    }


def reference(x_prompt, x_sample, state_hgrn, state_conv, hgrn_lb_logits, hgrn_w_in, hgrn_out_gain,
              hgrn_w_out, conv_w_in, conv_w, conv_w_out, norm_mix, norm_ffn, ffn_w_in, ffn_w_out,
              norm_final):
    lb_all = jnp.cumsum(jax.nn.softmax(hgrn_lb_logits.astype(jnp.float32), axis=0), axis=0)
    weights = (lb_all, hgrn_w_in, hgrn_out_gain, hgrn_w_out, conv_w_in, conv_w, conv_w_out,
               norm_mix, norm_ffn, ffn_w_in, ffn_w_out, norm_final)
    bp = x_prompt.shape[0]
    h0 = jnp.zeros((N_HGRN_LAYERS, bp, HGRN_HEADS, HGRN_DK, HGRN_DV), state_hgrn.dtype)
    c0 = jnp.zeros((N_CONV_LAYERS, bp, CONV_WIDTH - 1, D_MODEL), state_conv.dtype)
    y_prompt, hgrn_p, conv_p = _trunk(x_prompt, h0, c0, CHUNK, *weights)
    y_sample, hgrn_s, conv_s = _trunk(x_sample, state_hgrn, state_conv, x_sample.shape[1], *weights)
    return (y_prompt, y_sample, hgrn_p, hgrn_s, conv_p, conv_s)
```

```python
import functools

import jax
import jax.numpy as jnp
from jax import lax
from jax.experimental import pallas as pl
from jax.experimental.pallas import tpu as pltpu

F32 = jnp.float32
BF16 = jnp.bfloat16

NORM_EPS = 1e-6
HEAD_DIM = 128
PROMPT_CHUNK = 64
SUBLANES = 8
VMEM_LIMIT_BYTES = 56 * 1024 * 1024


def _params(semantics):
    return pltpu.CompilerParams(dimension_semantics=semantics,
                                vmem_limit_bytes=VMEM_LIMIT_BYTES)


def _rmsnorm_kernel(x_ref, g_ref, o_ref):
    x = x_ref[...]
    y = x * lax.rsqrt(jnp.mean(x * x, axis=-1, keepdims=True) + NORM_EPS)
    o_ref[...] = (y * g_ref[...]).astype(o_ref.dtype)


def _rmsnorm(x, gain, out_dtype):
    m, d = x.shape
    tm = min(256, m)
    return pl.pallas_call(
        _rmsnorm_kernel,
        grid=(m // tm,),
        in_specs=[pl.BlockSpec((tm, d), lambda i: (i, 0)),
                  pl.BlockSpec((1, d), lambda i: (0, 0))],
        out_specs=pl.BlockSpec((tm, d), lambda i: (i, 0)),
        out_shape=jax.ShapeDtypeStruct((m, d), out_dtype),
        compiler_params=_params(("arbitrary",)),
        name="rmsnorm",
    )(x, gain.reshape(1, d).astype(F32))


def _matmul_kernel(*refs, n_w, n_rows, n_tiles, epilogue):
    a_ref = refs[0]
    w_refs = refs[1:1 + n_w]
    row_refs = refs[1 + n_w:1 + n_w + n_rows]
    tile_refs = refs[1 + n_w + n_rows:1 + n_w + n_rows + n_tiles]
    out_refs = refs[1 + n_w + n_rows + n_tiles:]
    a = a_ref[...]
    zs = [jnp.dot(a, w_ref[...], preferred_element_type=F32) for w_ref in w_refs]
    res = epilogue(zs, [r[...] for r in row_refs], [t[...] for t in tile_refs])
    for o_ref, r in zip(out_refs, res):
        o_ref[...] = r.astype(o_ref.dtype)


def _matmul(a, w, col_offsets, n_cols, rows, tiles, out_dtypes, epilogue, tm, tn, name):
    m, k = a.shape
    tm = min(tm, m)
    nj = n_cols // tn
    assert m % tm == 0 and n_cols % tn == 0 and all(c % tn == 0 for c in col_offsets)
    in_specs = [pl.BlockSpec((tm, k), lambda i, j: (i, 0))]
    for c in col_offsets:
        in_specs.append(pl.BlockSpec((k, tn), lambda i, j, c=c: (0, c // tn + j)))
    for r in rows:
        in_specs.append(pl.BlockSpec((r.shape[0], tn), lambda i, j: (0, j)))
    for _ in tiles:
        in_specs.append(pl.BlockSpec((tm, tn), lambda i, j: (i, j)))
    out_specs = [pl.BlockSpec((tm, tn), lambda i, j: (i, j)) for _ in out_dtypes]
    out_shape = [jax.ShapeDtypeStruct((m, n_cols), dt) for dt in out_dtypes]
    kern = functools.partial(_matmul_kernel, n_w=len(col_offsets), n_rows=len(rows),
                             n_tiles=len(tiles), epilogue=epilogue)
    return pl.pallas_call(
        kern,
        grid=(m // tm, nj),
        in_specs=in_specs,
        out_specs=out_specs,
        out_shape=out_shape,
        compiler_params=_params(("arbitrary", "arbitrary")),
        name=name,
    )(a, *([w] * len(col_offsets)), *rows, *tiles)


def _silu(z):
    return z * jax.nn.sigmoid(z)


def _hgrn_proj_epilogue(zs, rows, tiles, *, layer):
    zq, zf, zi, zg = zs
    logits = rows[0]
    e = jnp.exp(logits - jnp.max(logits, axis=0, keepdims=True))
    lb = jnp.sum(e[:layer + 1], axis=0, keepdims=True) / jnp.sum(e, axis=0, keepdims=True)
    f = lb + (1.0 - lb) * jax.nn.sigmoid(zf)
    k = (1.0 - lb) * jax.nn.sigmoid(-zf)
    return [_silu(zq), k, jnp.log(f), zi, _silu(zg)]


def _residual_epilogue(zs, rows, tiles):
    return [tiles[0] + zs[0]]


def _swiglu_epilogue(zs, rows, tiles):
    return [_silu(zs[0]) * zs[1]]


def _conv_in_epilogue(zs, rows, tiles):
    gb, gc, u = zs
    return [gb, gc * u]


def _block_scans(g, chunk):
    row = lax.broadcasted_iota(jnp.int32, g.shape, 0)

    def down(x, d):
        return pltpu.roll(x, d, 0)

    def up(x, d):
        return pltpu.roll(x, chunk - d, 0)

    zero = jnp.zeros_like(g)
    scans = {1: (g, zero)}
    p, x = g, zero
    m = 1
    while m < SUBLANES:
        pos = row & (2 * m - 1)
        f = g + x
        p_next, x_next = p, x
        for j in range(m):
            p_next = p_next + jnp.where(pos == m + j, down(p, j + 1), 0.0)
            x_next = x_next + jnp.where(pos == m - 1 - j, up(f, j + 1), 0.0)
        p, x = p_next, x_next
        m *= 2
        scans[m] = (p, x)
    n_tiles = chunk // SUBLANES
    p8 = [p[SUBLANES * j:SUBLANES * (j + 1)] for j in range(n_tiles)]
    x8 = [x[SUBLANES * j:SUBLANES * (j + 1)] for j in range(n_tiles)]
    tot = [jnp.sum(g[SUBLANES * j:SUBLANES * (j + 1)], axis=0, keepdims=True) for j in range(n_tiles)]
    while m < chunk:
        m *= 2
        per = m // SUBLANES
        ps, xs = [], []
        for j in range(n_tiles):
            lo = (j // per) * per
            before = None
            for i in range(lo, j):
                before = tot[i] if before is None else before + tot[i]
            after = None
            for i in range(j + 1, lo + per):
                after = tot[i] if after is None else after + tot[i]
            ps.append(p8[j] if before is None else p8[j] + before)
            xs.append(x8[j] if after is None else x8[j] + after)
        scans[m] = (jnp.concatenate(ps, axis=0), jnp.concatenate(xs, axis=0))
    return scans


def _hgrn_kernel(q_ref, k_ref, g_ref, v_ref, z_ref, s0_ref, gain_ref, o_ref, sfin_ref, st_ref,
                 *, chunk, n_chunks):
    l = pl.program_id(2)

    @pl.when(l == 0)
    def _():
        st_ref[...] = s0_ref[0, 0].T

    t_idx = lax.broadcasted_iota(jnp.int32, (chunk, chunk), 0)
    s_idx = lax.broadcasted_iota(jnp.int32, (chunk, chunk), 1)
    differ = t_idx ^ s_idx
    gain = gain_ref[...]
    nt_dims = (((1,), (1,)), ((), ()))
    tn_dims = (((0,), (0,)), ((), ()))

    def body(c, carry):
        r0 = pl.multiple_of(c * chunk, chunk)
        rows = pl.ds(r0, chunk)
        q = q_ref[0, rows, :]
        k = k_ref[0, rows, :]
        g = g_ref[0, rows, :]
        v = v_ref[0, rows, :]
        scans = _block_scans(g, chunk)

        a = jnp.where(t_idx == s_idx,
                      lax.dot_general(q.astype(BF16), k.astype(BF16), nt_dims,
                                      preferred_element_type=F32), 0.0)
        m = 1
        while m < chunk:
            p_m, x_m = scans[m]
            qm = (q * jnp.exp(p_m)).astype(BF16)
            km = (k * jnp.exp(x_m)).astype(BF16)
            a_m = lax.dot_general(qm, km, nt_dims, preferred_element_type=F32)
            level = (t_idx > s_idx) & (differ >= m) & (differ < 2 * m)
            a = jnp.where(level, a_m, a)
            m *= 2

        p_c, x_c = scans[chunk]
        st = st_ref[...]
        q_in = (q * jnp.exp(p_c)).astype(BF16)
        o = lax.dot_general(q_in, st.astype(BF16), nt_dims, preferred_element_type=F32)
        o = o + jnp.dot(a.astype(BF16), v, preferred_element_type=F32)
        k_out = (k * jnp.exp(x_c)).astype(BF16)
        decay = jnp.exp(p_c[chunk - 1:chunk, :])
        st_ref[...] = decay * st + lax.dot_general(v, k_out, tn_dims, preferred_element_type=F32)

        o = o * lax.rsqrt(jnp.mean(o * o, axis=-1, keepdims=True) + NORM_EPS) * gain
        o_ref[0, rows, :] = (o * z_ref[0, rows, :]).astype(o_ref.dtype)
        return carry

    lax.fori_loop(0, n_chunks, body, 0)

    @pl.when(l == pl.num_programs(2) - 1)
    def _():
        sfin_ref[0, 0] = st_ref[...].T


def _hgrn_recurrence(q, k, g, v, z, s0, gain, chunk, block_frames):
    b, l, d = q.shape
    h = d // HEAD_DIM
    t = min(block_frames, l)
    assert l % t == 0 and t % chunk == 0
    seq = pl.BlockSpec((1, t, HEAD_DIM), lambda bi, hi, li: (bi, li, hi))
    state = pl.BlockSpec((1, 1, HEAD_DIM, HEAD_DIM), lambda bi, hi, li: (bi, hi, 0, 0))
    kern = functools.partial(_hgrn_kernel, chunk=chunk, n_chunks=t // chunk)
    return pl.pallas_call(
        kern,
        grid=(b, h, l // t),
        in_specs=[seq, seq, seq, seq, seq, state,
                  pl.BlockSpec((1, HEAD_DIM), lambda bi, hi, li: (0, 0))],
        out_specs=[seq, state],
        out_shape=[jax.ShapeDtypeStruct((b, l, d), BF16),
                   jax.ShapeDtypeStruct(s0.shape, s0.dtype)],
        scratch_shapes=[pltpu.VMEM((HEAD_DIM, HEAD_DIM), F32)],
        compiler_params=_params(("arbitrary", "arbitrary", "arbitrary")),
        name="hgrn_recurrence",
    )(q, k, g, v, z, s0, gain.reshape(1, HEAD_DIM).astype(F32))


def _conv_kernel(gb_ref, cu_ref, buf_ref, w_ref, o_ref, state_ref, carry_ref):
    l = pl.program_id(2)

    @pl.when(l == 0)
    def _():
        carry_ref[...] = buf_ref[0]

    cu = cu_ref[0]
    tl = cu.shape[0]
    row = lax.broadcasted_iota(jnp.int32, cu.shape, 0)
    prev2 = carry_ref[0:1, :]
    prev1 = carry_ref[1:2, :]
    c1 = jnp.where(row == 0, prev1, pltpu.roll(cu, 1, 0))
    c2 = jnp.where(row == 0, prev2, jnp.where(row == 1, prev1, pltpu.roll(cu, 2, 0)))
    conv = w_ref[0:1, :] * c2 + w_ref[1:2, :] * c1 + w_ref[2:3, :] * cu
    o_ref[0] = (gb_ref[0] * conv).astype(o_ref.dtype)
    last = cu[tl - 2:tl, :]
    carry_ref[...] = last
    state_ref[0] = last


def _short_conv(gb, cu, buf, w, block_frames, td=512):
    b, l, d = cu.shape
    tl = min(block_frames, l)
    assert l % tl == 0 and d % td == 0 and tl >= 2
    seq = pl.BlockSpec((1, tl, td), lambda bi, di, li: (bi, li, di))
    two = pl.BlockSpec((1, 2, td), lambda bi, di, li: (bi, 0, di))
    return pl.pallas_call(
        _conv_kernel,
        grid=(b, d // td, l // tl),
        in_specs=[seq, seq, two, pl.BlockSpec((3, td), lambda bi, di, li: (0, di))],
        out_specs=[seq, two],
        out_shape=[jax.ShapeDtypeStruct((b, l, d), BF16),
                   jax.ShapeDtypeStruct(buf.shape, buf.dtype)],
        scratch_shapes=[pltpu.VMEM((2, td), F32)],
        compiler_params=_params(("arbitrary", "arbitrary", "arbitrary")),
        name="short_conv",
    )(gb, cu, buf, w)


def _ffn(x, norm_gain, w_in, w_out, d_ff):
    h = _rmsnorm(x, norm_gain, BF16)
    (act,) = _matmul(h, w_in, (0, d_ff), d_ff, [], [], [BF16], _swiglu_epilogue,
                     tm=1024, tn=256, name="ffn_in")
    (x,) = _matmul(act, w_out, (0,), w_out.shape[1], [], [x], [F32], _residual_epilogue,
                   tm=512, tn=256, name="ffn_out")
    return x


def _trunk(x, hgrn_state, conv_state, chunk, lb_logits, hgrn_w_in, hgrn_out_gain, hgrn_w_out,
           conv_w_in, conv_w, conv_w_out, norm_mix, norm_ffn, ffn_w_in, ffn_w_out, norm_final):
    b, l, d = x.shape
    m = b * l
    d_ff = ffn_w_out.shape[1]
    x = x.reshape(m, d)

    h = _rmsnorm(x, norm_mix[0], BF16)
    q, k, g, v, z = _matmul(
        h, hgrn_w_in[0], (0, d, 2 * d, 3 * d), d, [lb_logits.astype(F32)], [],
        [F32, F32, F32, BF16, F32], functools.partial(_hgrn_proj_epilogue, layer=0),
        tm=512, tn=256, name="hgrn_proj")
    seq = lambda t: t.reshape(b, l, d)
    o, new_h = _hgrn_recurrence(seq(q), seq(k), seq(g), seq(v), seq(z), hgrn_state[0],
                                hgrn_out_gain[0], chunk, block_frames=512)
    (x,) = _matmul(o.reshape(m, d), hgrn_w_out[0], (0,), d, [], [x], [F32], _residual_epilogue,
                   tm=1024, tn=512, name="hgrn_out")
    x = _ffn(x, norm_ffn[0], ffn_w_in[0], ffn_w_out[0], d_ff)

    h = _rmsnorm(x, norm_mix[1], BF16)
    gb, cu = _matmul(h, conv_w_in[0], (0, d, 2 * d), d, [], [], [F32, F32], _conv_in_epilogue,
                     tm=1024, tn=256, name="conv_in")
    gated, new_c = _short_conv(seq(gb), seq(cu), conv_state[0], conv_w[0].astype(F32),
                               block_frames=1024)
    (x,) = _matmul(gated.reshape(m, d), conv_w_out[0], (0,), d, [], [x], [F32], _residual_epilogue,
                   tm=1024, tn=512, name="conv_out")
    x = _ffn(x, norm_ffn[1], ffn_w_in[1], ffn_w_out[1], d_ff)

    y = _rmsnorm(x, norm_final, F32).reshape(b, l, d)
    return y, new_h[None], new_c[None]


def kernel(x_prompt, x_sample, state_hgrn, state_conv, hgrn_lb_logits, hgrn_w_in, hgrn_out_gain,
           hgrn_w_out, conv_w_in, conv_w, conv_w_out, norm_mix, norm_ffn, ffn_w_in, ffn_w_out,
           norm_final):
    weights = (hgrn_lb_logits, hgrn_w_in.astype(BF16), hgrn_out_gain, hgrn_w_out.astype(BF16),
               conv_w_in.astype(BF16), conv_w, conv_w_out.astype(BF16), norm_mix, norm_ffn,
               ffn_w_in.astype(BF16), ffn_w_out.astype(BF16), norm_final)
    bp = x_prompt.shape[0]
    h0 = jnp.zeros((state_hgrn.shape[0], bp) + state_hgrn.shape[2:], state_hgrn.dtype)
    c0 = jnp.zeros((state_conv.shape[0], bp) + state_conv.shape[2:], state_conv.dtype)
    y_prompt, hgrn_p, conv_p = _trunk(x_prompt, h0, c0, PROMPT_CHUNK, *weights)
    y_sample, hgrn_s, conv_s = _trunk(x_sample, state_hgrn, state_conv, x_sample.shape[1], *weights)
    return (y_prompt, y_sample, hgrn_p, hgrn_s, conv_p, conv_s)
```

```python
import functools

import jax
import jax.numpy as jnp
from jax import lax
from jax.experimental import pallas as pl
from jax.experimental.pallas import tpu as pltpu

F32 = jnp.float32
BF16 = jnp.bfloat16

NORM_EPS = 1e-6
HEAD_DIM = 128
PROMPT_CHUNK = 64
SUBLANES = 8
HEADS_PER_STEP = 4
VMEM_LIMIT_BYTES = 56 * 1024 * 1024


def _params(semantics):
    return pltpu.CompilerParams(dimension_semantics=semantics,
                                vmem_limit_bytes=VMEM_LIMIT_BYTES)


def _rmsnorm_kernel(x_ref, g_ref, o_ref):
    x = x_ref[...]
    y = x * lax.rsqrt(jnp.mean(x * x, axis=-1, keepdims=True) + NORM_EPS)
    o_ref[...] = (y * g_ref[...]).astype(o_ref.dtype)


def _rmsnorm(x, gain, out_dtype):
    m, d = x.shape
    tm = min(256, m)
    return pl.pallas_call(
        _rmsnorm_kernel,
        grid=(m // tm,),
        in_specs=[pl.BlockSpec((tm, d), lambda i: (i, 0)),
                  pl.BlockSpec((1, d), lambda i: (0, 0))],
        out_specs=pl.BlockSpec((tm, d), lambda i: (i, 0)),
        out_shape=jax.ShapeDtypeStruct((m, d), out_dtype),
        compiler_params=_params(("arbitrary",)),
        name="rmsnorm",
    )(x, gain.reshape(1, d).astype(F32))


def _silu(z):
    return z * jax.nn.sigmoid(z)


def _hgrn_proj_epilogue(zs, rows, tiles, *, layer):
    zq, zf, zi, zg = zs
    logits = rows[0]
    e = jnp.exp(logits - jnp.max(logits, axis=0, keepdims=True))
    lb = jnp.sum(e[:layer + 1], axis=0, keepdims=True) / jnp.sum(e, axis=0, keepdims=True)
    f = lb + (1.0 - lb) * jax.nn.sigmoid(zf)
    k = (1.0 - lb) * jax.nn.sigmoid(-zf)
    return [_silu(zq), k, jnp.log(f), zi, _silu(zg)]


def _residual_epilogue(zs, rows, tiles):
    return [tiles[0] + zs[0]]


def _swiglu_epilogue(zs, rows, tiles):
    return [_silu(zs[0]) * zs[1]]


def _conv_in_epilogue(zs, rows, tiles):
    gb, gc, u = zs
    return [gb, gc * u]


def _ws_matmul_kernel(*refs, n_w, n_rows, n_tiles, n_out, tn, epilogue):
    it = iter(refs)
    take = lambda n: [next(it) for _ in range(n)]
    (a_p, a_s), w_refs, row_refs = take(2), take(n_w), take(n_rows)
    tiles_p, tiles_s, outs_p, outs_s = take(n_tiles), take(n_tiles), take(n_out), take(n_out)
    (wcat,) = take(1)

    def compute(a_ref, tile_refs, out_refs):
        z = jnp.dot(a_ref[...], wcat[...], preferred_element_type=F32)
        zs = [z[:, c * tn:(c + 1) * tn] for c in range(n_w)]
        res = epilogue(zs, [r[...] for r in row_refs], [t[...] for t in tile_refs])
        for o_ref, r in zip(out_refs, res):
            o_ref[...] = r.astype(o_ref.dtype).reshape(o_ref.shape)

    @pl.when(pl.program_id(1) == 0)
    def _():
        for c, w_ref in enumerate(w_refs):
            wcat[:, c * tn:(c + 1) * tn] = w_ref[...].astype(BF16)
        compute(a_s, tiles_s, outs_s)

    compute(a_p, tiles_p, outs_p)


def _ws_matmul(a_p, a_s, w, col_offsets, n_cols, rows, tiles_p, tiles_s, out_dtypes, epilogue,
               tm, tn, name, head_major=False):
    m_p, k = a_p.shape
    m_s = a_s.shape[0]
    tm = min(tm, m_p)
    tn = min(tn, n_cols)
    nj = n_cols // tn
    n_w = len(col_offsets)
    assert m_p % tm == 0 and n_cols % tn == 0 and all(c % tn == 0 for c in col_offsets)
    in_specs = [pl.BlockSpec((tm, k), lambda j, i: (i, 0)),
                pl.BlockSpec((m_s, k), lambda j, i: (0, 0))]
    for c in col_offsets:
        in_specs.append(pl.BlockSpec((k, tn), lambda j, i, c=c: (0, c // tn + j)))
    for r in rows:
        in_specs.append(pl.BlockSpec((r.shape[0], tn), lambda j, i: (0, j)))
    in_specs += [pl.BlockSpec((tm, tn), lambda j, i: (i, j)) for _ in tiles_p]
    in_specs += [pl.BlockSpec((m_s, tn), lambda j, i: (0, j)) for _ in tiles_s]
    if head_major:
        spec_p = pl.BlockSpec((1, tm, tn), lambda j, i: (j, i, 0))
        spec_s = pl.BlockSpec((1, m_s, tn), lambda j, i: (j, 0, 0))
        shape = lambda m: (nj, m, tn)
    else:
        spec_p = pl.BlockSpec((tm, tn), lambda j, i: (i, j))
        spec_s = pl.BlockSpec((m_s, tn), lambda j, i: (0, j))
        shape = lambda m: (m, n_cols)
    n_out = len(out_dtypes)
    out_specs = [spec_p] * n_out + [spec_s] * n_out
    out_shape = ([jax.ShapeDtypeStruct(shape(m_p), dt) for dt in out_dtypes]
                 + [jax.ShapeDtypeStruct(shape(m_s), dt) for dt in out_dtypes])
    kern = functools.partial(_ws_matmul_kernel, n_w=n_w, n_rows=len(rows), n_tiles=len(tiles_p),
                             n_out=n_out, tn=tn, epilogue=epilogue)
    res = pl.pallas_call(
        kern,
        grid=(nj, m_p // tm),
        in_specs=in_specs,
        out_specs=out_specs,
        out_shape=out_shape,
        scratch_shapes=[pltpu.VMEM((k, n_w * tn), BF16)],
        compiler_params=_params(("arbitrary", "arbitrary")),
        name=name,
    )(a_p, a_s, *([w] * n_w), *rows, *tiles_p, *tiles_s)
    return res[:n_out], res[n_out:]


def _matmul_kernel(a_ref, w_ref, x_ref, o_ref):
    o_ref[...] = x_ref[...] + jnp.dot(a_ref[...], w_ref[...], preferred_element_type=F32)


def _matmul_residual(a, w, x, tm, tn, name):
    m, k = a.shape
    n = w.shape[1]
    tm = min(tm, m)
    tn = min(tn, n)
    assert m % tm == 0 and n % tn == 0
    return pl.pallas_call(
        _matmul_kernel,
        grid=(m // tm, n // tn),
        in_specs=[pl.BlockSpec((tm, k), lambda i, j: (i, 0)),
                  pl.BlockSpec((k, tn), lambda i, j: (0, j)),
                  pl.BlockSpec((tm, tn), lambda i, j: (i, j))],
        out_specs=pl.BlockSpec((tm, tn), lambda i, j: (i, j)),
        out_shape=jax.ShapeDtypeStruct((m, n), F32),
        compiler_params=_params(("arbitrary", "arbitrary")),
        name=name,
    )(a, w, x)


def _neg_abs(x):
    bits = lax.bitcast_convert_type(x, jnp.uint32) | jnp.uint32(0x80000000)
    return lax.bitcast_convert_type(bits, F32)


def _decay_exponents(g, p_ref, chunk):
    n = chunk // SUBLANES
    r8 = lax.broadcasted_iota(jnp.int32, (SUBLANES, HEAD_DIM), 0)
    tiles = [g[SUBLANES * j:SUBLANES * (j + 1)] for j in range(n)]
    pre = []
    for t in tiles:
        p = t
        d = 1
        while d < SUBLANES:
            p = p + jnp.where(r8 >= d, pltpu.roll(p, d, 0), 0.0)
            d *= 2
        pre.append(p)
    p_ref[...] = jnp.concatenate(pre, axis=0)

    def row(i):
        return jnp.broadcast_to(p_ref[i:i + 1, :], (SUBLANES, HEAD_DIM))

    tot = [row(SUBLANES * j + SUBLANES - 1) for j in range(n)]
    suf = [tot[j] - pre[j] for j in range(n)]
    end = [tot[0]]
    for j in range(1, n):
        end.append(end[j - 1] + tot[j])

    def upper(j, jm):
        return pre[j] if j - 1 == jm else pre[j] + (end[j - 1] - end[jm])

    def lower(j, jm):
        return suf[j] if j == jm else suf[j] + (end[jm] - end[j])

    levels = {1: jnp.concatenate([jnp.where((r8 & 1) == 1, t, 0.0) for t in tiles], axis=0)}
    m = 2
    while m < SUBLANES:
        out = []
        for j in range(n):
            base = SUBLANES * j
            mid = row(base + m - 1)
            for blk in range(1, SUBLANES // (2 * m)):
                mid = jnp.where(r8 < 2 * m * blk, mid, row(base + 2 * m * blk + m - 1))
            out.append(_neg_abs(pre[j] - mid))
        levels[m] = jnp.concatenate(out, axis=0)
        m *= 2
    while m < chunk:
        per = 2 * m // SUBLANES
        out = []
        for j in range(n):
            jm = (j // per) * per + per // 2 - 1
            out.append(upper(j, jm) if j > jm else lower(j, jm))
        levels[m] = jnp.concatenate(out, axis=0)
        m *= 2
    p_c = jnp.concatenate([pre[0]] + [pre[j] + end[j - 1] for j in range(1, n)], axis=0)
    x_c = jnp.concatenate([lower(j, n - 1) for j in range(n)], axis=0)
    return levels, p_c, x_c


def _hgrn_kernel(q_ref, k_ref, g_ref, v_ref, z_ref, s0_ref, gain_ref, o_ref, sfin_ref, st_ref,
                 p_ref, *, chunk, n_chunks, heads):
    l = pl.program_id(2)

    @pl.when(l == 0)
    def _():
        for hb in range(heads):
            st_ref[hb] = s0_ref[0, hb].T

    t_idx = lax.broadcasted_iota(jnp.int32, (chunk, chunk), 0)
    s_idx = lax.broadcasted_iota(jnp.int32, (chunk, chunk), 1)
    differ = t_idx ^ s_idx
    diagonal = t_idx == s_idx
    masks = []
    m = 1
    while m < chunk:
        masks.append((m, (t_idx > s_idx) & (differ >= m) & (differ < 2 * m)))
        m *= 2
    gain = gain_ref[...]
    nt_dims = (((1,), (1,)), ((), ()))
    tn_dims = (((0,), (0,)), ((), ()))

    def body(c, carry):
        r0 = pl.multiple_of(c * chunk, chunk)
        rows = pl.ds(r0, chunk)
        for hb in range(heads):
            lanes = slice(hb * HEAD_DIM, (hb + 1) * HEAD_DIM)
            q = q_ref[hb, rows, :]
            k = k_ref[hb, rows, :]
            v = v_ref[hb, rows, :]
            levels, p_c, x_c = _decay_exponents(g_ref[hb, rows, :], p_ref.at[hb], chunk)

            a = jnp.where(diagonal,
                          lax.dot_general(q.astype(BF16), k.astype(BF16), nt_dims,
                                          preferred_element_type=F32), 0.0)
            for m, mask in masks:
                e = jnp.exp(levels[m])
                qm = (q * e).astype(BF16)
                km = (k * e).astype(BF16)
                a = jnp.where(mask, lax.dot_general(qm, km, nt_dims, preferred_element_type=F32), a)

            st = st_ref[hb]
            q_in = (q * jnp.exp(p_c)).astype(BF16)
            o = lax.dot_general(q_in, st.astype(BF16), nt_dims, preferred_element_type=F32)
            o = o + jnp.dot(a.astype(BF16), v, preferred_element_type=F32)
            k_out = (k * jnp.exp(x_c)).astype(BF16)
            decay = jnp.exp(p_c[chunk - 1:chunk, :])
            st_ref[hb] = decay * st + lax.dot_general(v, k_out, tn_dims, preferred_element_type=F32)

            o = o * lax.rsqrt(jnp.mean(o * o, axis=-1, keepdims=True) + NORM_EPS) * gain
            o_ref[rows, lanes] = (o * z_ref[hb, rows, :]).astype(o_ref.dtype)
        return carry

    lax.fori_loop(0, n_chunks, body, 0)

    @pl.when(l == pl.num_programs(2) - 1)
    def _():
        for hb in range(heads):
            sfin_ref[0, hb] = st_ref[hb].T


def _hgrn_recurrence(q, k, g, v, z, s0, gain, n_streams, chunk, block_frames):
    h, m, _ = q.shape
    l = m // n_streams
    t = min(block_frames, l)
    hb = min(HEADS_PER_STEP, h)
    assert l % t == 0 and t % chunk == 0 and h % hb == 0
    nl = l // t
    seq = pl.BlockSpec((hb, t, HEAD_DIM), lambda bi, hi, li: (hi, bi * nl + li, 0))
    state = pl.BlockSpec((1, hb, HEAD_DIM, HEAD_DIM), lambda bi, hi, li: (bi, hi, 0, 0))
    kern = functools.partial(_hgrn_kernel, chunk=chunk, n_chunks=t // chunk, heads=hb)
    return pl.pallas_call(
        kern,
        grid=(n_streams, h // hb, nl),
        in_specs=[seq, seq, seq, seq, seq, state,
                  pl.BlockSpec((1, HEAD_DIM), lambda bi, hi, li: (0, 0))],
        out_specs=[pl.BlockSpec((t, hb * HEAD_DIM), lambda bi, hi, li: (bi * nl + li, hi)), state],
        out_shape=[jax.ShapeDtypeStruct((m, h * HEAD_DIM), BF16),
                   jax.ShapeDtypeStruct(s0.shape, s0.dtype)],
        scratch_shapes=[pltpu.VMEM((hb, HEAD_DIM, HEAD_DIM), F32),
                        pltpu.VMEM((hb, chunk, HEAD_DIM), F32)],
        compiler_params=_params(("arbitrary", "arbitrary", "arbitrary")),
        name="hgrn_recurrence",
    )(q, k, g, v, z, s0, gain.reshape(1, HEAD_DIM).astype(F32))


def _conv_kernel(gb_ref, cu_ref, buf_ref, w_ref, o_ref, state_ref, carry_ref):
    l = pl.program_id(2)

    @pl.when(l == 0)
    def _():
        carry_ref[...] = buf_ref[0]

    cu = cu_ref[0]
    tl = cu.shape[0]
    row = lax.broadcasted_iota(jnp.int32, cu.shape, 0)
    prev2 = carry_ref[0:1, :]
    prev1 = carry_ref[1:2, :]
    c1 = jnp.where(row == 0, prev1, pltpu.roll(cu, 1, 0))
    c2 = jnp.where(row == 0, prev2, jnp.where(row == 1, prev1, pltpu.roll(cu, 2, 0)))
    conv = w_ref[0:1, :] * c2 + w_ref[1:2, :] * c1 + w_ref[2:3, :] * cu
    o_ref[0] = (gb_ref[0] * conv).astype(o_ref.dtype)
    last = cu[tl - 2:tl, :]
    carry_ref[...] = last
    state_ref[0] = last


def _short_conv(gb, cu, buf, w, block_frames, td=512):
    b, l, d = cu.shape
    tl = min(block_frames, l)
    td = min(td, d)
    assert l % tl == 0 and d % td == 0 and tl >= 2
    seq = pl.BlockSpec((1, tl, td), lambda bi, di, li: (bi, li, di))
    two = pl.BlockSpec((1, 2, td), lambda bi, di, li: (bi, 0, di))
    return pl.pallas_call(
        _conv_kernel,
        grid=(b, d // td, l // tl),
        in_specs=[seq, seq, two, pl.BlockSpec((3, td), lambda bi, di, li: (0, di))],
        out_specs=[seq, two],
        out_shape=[jax.ShapeDtypeStruct((b, l, d), BF16),
                   jax.ShapeDtypeStruct(buf.shape, buf.dtype)],
        scratch_shapes=[pltpu.VMEM((2, td), F32)],
        compiler_params=_params(("arbitrary", "arbitrary", "arbitrary")),
        name="short_conv",
    )(gb, cu, buf, w)


def _ffn(x_p, x_s, norm_gain, w_in, w_out_bf16, d_ff):
    h_p = _rmsnorm(x_p, norm_gain, BF16)
    h_s = _rmsnorm(x_s, norm_gain, BF16)
    (act_p,), (act_s,) = _ws_matmul(h_p, h_s, w_in, (0, d_ff), d_ff, [], [], [], [BF16],
                                    _swiglu_epilogue, tm=1024, tn=256, name="ffn_in")
    x_p = _matmul_residual(act_p, w_out_bf16, x_p, tm=512, tn=256, name="ffn_out")
    x_s = _matmul_residual(act_s, w_out_bf16, x_s, tm=512, tn=256, name="ffn_out")
    return x_p, x_s


def kernel(x_prompt, x_sample, state_hgrn, state_conv, hgrn_lb_logits, hgrn_w_in, hgrn_out_gain,
           hgrn_w_out, conv_w_in, conv_w, conv_w_out, norm_mix, norm_ffn, ffn_w_in, ffn_w_out,
           norm_final):
    bp, lp, d = x_prompt.shape
    bs, ls, _ = x_sample.shape
    d_ff = ffn_w_out.shape[1]
    x_p = x_prompt.reshape(bp * lp, d)
    x_s = x_sample.reshape(bs * ls, d)
    ffn_w_out_bf16 = ffn_w_out.astype(BF16)
    logits = hgrn_lb_logits.astype(F32)

    h_p = _rmsnorm(x_p, norm_mix[0], BF16)
    h_s = _rmsnorm(x_s, norm_mix[0], BF16)
    proj_p, proj_s = _ws_matmul(
        h_p, h_s, hgrn_w_in[0], (0, d, 2 * d, 3 * d), d, [logits], [], [],
        [F32, F32, F32, BF16, F32], functools.partial(_hgrn_proj_epilogue, layer=0),
        tm=512, tn=HEAD_DIM, name="hgrn_proj", head_major=True)
    zero_state = jnp.zeros((bp,) + state_hgrn.shape[2:], state_hgrn.dtype)
    o_p, hgrn_p = _hgrn_recurrence(*proj_p, zero_state, hgrn_out_gain[0], bp, PROMPT_CHUNK,
                                   block_frames=512)
    o_s, hgrn_s = _hgrn_recurrence(*proj_s, state_hgrn[0], hgrn_out_gain[0], bs, ls,
                                   block_frames=ls)
    (x_p,), (x_s,) = _ws_matmul(o_p, o_s, hgrn_w_out[0], (0,), d, [], [x_p], [x_s], [F32],
                                _residual_epilogue, tm=512, tn=512, name="hgrn_out")
    x_p, x_s = _ffn(x_p, x_s, norm_ffn[0], ffn_w_in[0], ffn_w_out_bf16[0], d_ff)

    h_p = _rmsnorm(x_p, norm_mix[1], BF16)
    h_s = _rmsnorm(x_s, norm_mix[1], BF16)
    (gb_p, cu_p), (gb_s, cu_s) = _ws_matmul(
        h_p, h_s, conv_w_in[0], (0, d, 2 * d), d, [], [], [], [F32, F32], _conv_in_epilogue,
        tm=512, tn=256, name="conv_in")
    w_conv = conv_w[0].astype(F32)
    zero_buf = jnp.zeros((bp,) + state_conv.shape[2:], state_conv.dtype)
    gated_p, conv_p = _short_conv(gb_p.reshape(bp, lp, d), cu_p.reshape(bp, lp, d), zero_buf,
                                  w_conv, block_frames=1024)
    gated_s, conv_s = _short_conv(gb_s.reshape(bs, ls, d), cu_s.reshape(bs, ls, d), state_conv[0],
                                  w_conv, block_frames=ls)
    (x_p,), (x_s,) = _ws_matmul(gated_p.reshape(bp * lp, d), gated_s.reshape(bs * ls, d),
                                conv_w_out[0], (0,), d, [], [x_p], [x_s], [F32],
                                _residual_epilogue, tm=512, tn=512, name="conv_out")
    x_p, x_s = _ffn(x_p, x_s, norm_ffn[1], ffn_w_in[1], ffn_w_out_bf16[1], d_ff)

    y_p = _rmsnorm(x_p, norm_final, F32).reshape(bp, lp, d)
    y_s = _rmsnorm(x_s, norm_final, F32).reshape(bs, ls, d)
    return (y_p, y_s, hgrn_p[None], hgrn_s[None], conv_p[None], conv_s[None])
```

```python
import functools

import jax
import jax.numpy as jnp
from jax import lax
from jax.experimental import pallas as pl
from jax.experimental.pallas import tpu as pltpu

F32 = jnp.float32
BF16 = jnp.bfloat16

NORM_EPS = 1e-6
HEAD_DIM = 128
PROMPT_CHUNK = 64
SUBLANES = 8
HEADS_PER_STEP = 4
ROW_SUBBLOCK = 256
VMEM_LIMIT_BYTES = 56 * 1024 * 1024


def _params(semantics):
    return pltpu.CompilerParams(dimension_semantics=semantics,
                                vmem_limit_bytes=VMEM_LIMIT_BYTES)


def _rmsnorm_kernel(x_ref, g_ref, o_ref):
    x = x_ref[...]
    y = x * lax.rsqrt(jnp.mean(x * x, axis=-1, keepdims=True) + NORM_EPS)
    o_ref[...] = (y * g_ref[...]).astype(o_ref.dtype)


def _rmsnorm(x, gain, out_dtype):
    m, d = x.shape
    tm = min(256, m)
    return pl.pallas_call(
        _rmsnorm_kernel,
        grid=(m // tm,),
        in_specs=[pl.BlockSpec((tm, d), lambda i: (i, 0)),
                  pl.BlockSpec((1, d), lambda i: (0, 0))],
        out_specs=pl.BlockSpec((tm, d), lambda i: (i, 0)),
        out_shape=jax.ShapeDtypeStruct((m, d), out_dtype),
        compiler_params=_params(("arbitrary",)),
        name="rmsnorm",
    )(x, gain.reshape(1, d).astype(F32))


def _silu(z):
    return z * jax.nn.sigmoid(z)


def _hgrn_proj_epilogue(zs, rows, tiles, *, layer):
    zq, zf, zi, zg = zs
    logits = rows[0]
    e = jnp.exp(logits - jnp.max(logits, axis=0, keepdims=True))
    lb = jnp.sum(e[:layer + 1], axis=0, keepdims=True) / jnp.sum(e, axis=0, keepdims=True)
    f = lb + (1.0 - lb) * jax.nn.sigmoid(zf)
    k = (1.0 - lb) * jax.nn.sigmoid(-zf)
    return [_silu(zq), k, jnp.log(f), zi, _silu(zg)]


def _residual_epilogue(zs, rows, tiles):
    return [tiles[0] + zs[0]]


def _swiglu_epilogue(zs, rows, tiles):
    return [_silu(zs[0]) * zs[1]]


def _conv_in_epilogue(zs, rows, tiles):
    gb, gc, u = zs
    return [gb, gc * u]


def _ws_matmul_kernel(*refs, n_w, n_rows, n_tiles, n_out, tn, epilogue):
    it = iter(refs)
    take = lambda n: [next(it) for _ in range(n)]
    (a_p, a_s), w_refs, row_refs = take(2), take(n_w), take(n_rows)
    tiles_p, tiles_s, outs_p, outs_s = take(n_tiles), take(n_tiles), take(n_out), take(n_out)
    (wcat,) = take(1)

    def compute(a_ref, tile_refs, out_refs):
        m = a_ref.shape[0]
        sub = min(ROW_SUBBLOCK, m)
        for r0 in range(0, m, sub):
            rs = slice(r0, r0 + sub)
            z = jnp.dot(a_ref[rs, :], wcat[...], preferred_element_type=F32)
            zs = [z[:, c * tn:(c + 1) * tn] for c in range(n_w)]
            res = epilogue(zs, [r[...] for r in row_refs], [t[rs, :] for t in tile_refs])
            for o_ref, r in zip(out_refs, res):
                if len(o_ref.shape) == 3:
                    o_ref[0, rs, :] = r.astype(o_ref.dtype)
                else:
                    o_ref[rs, :] = r.astype(o_ref.dtype)

    @pl.when(pl.program_id(1) == 0)
    def _():
        for c, w_ref in enumerate(w_refs):
            wcat[:, c * tn:(c + 1) * tn] = w_ref[...].astype(BF16)
        compute(a_s, tiles_s, outs_s)

    compute(a_p, tiles_p, outs_p)


def _ws_matmul(a_p, a_s, w, layer, col_offsets, n_cols, rows, tiles_p, tiles_s, out_dtypes,
               epilogue, tm, tn, name, head_major=False):
    m_p, k = a_p.shape
    m_s = a_s.shape[0]
    tm = min(tm, m_p)
    tn = min(tn, n_cols)
    nj = n_cols // tn
    n_w = len(col_offsets)
    assert m_p % tm == 0 and n_cols % tn == 0 and all(c % tn == 0 for c in col_offsets)
    in_specs = [pl.BlockSpec((tm, k), lambda j, i: (i, 0)),
                pl.BlockSpec((m_s, k), lambda j, i: (0, 0))]
    for c in col_offsets:
        in_specs.append(pl.BlockSpec((None, k, tn), lambda j, i, c=c: (layer, 0, c // tn + j)))
    for r in rows:
        in_specs.append(pl.BlockSpec((r.shape[0], tn), lambda j, i: (0, j)))
    in_specs += [pl.BlockSpec((tm, tn), lambda j, i: (i, j)) for _ in tiles_p]
    in_specs += [pl.BlockSpec((m_s, tn), lambda j, i: (0, j)) for _ in tiles_s]
    if head_major:
        spec_p = pl.BlockSpec((1, tm, tn), lambda j, i: (j, i, 0))
        spec_s = pl.BlockSpec((1, m_s, tn), lambda j, i: (j, 0, 0))
        shape = lambda m: (nj, m, tn)
    else:
        spec_p = pl.BlockSpec((tm, tn), lambda j, i: (i, j))
        spec_s = pl.BlockSpec((m_s, tn), lambda j, i: (0, j))
        shape = lambda m: (m, n_cols)
    n_out = len(out_dtypes)
    out_specs = [spec_p] * n_out + [spec_s] * n_out
    out_shape = ([jax.ShapeDtypeStruct(shape(m_p), dt) for dt in out_dtypes]
                 + [jax.ShapeDtypeStruct(shape(m_s), dt) for dt in out_dtypes])
    kern = functools.partial(_ws_matmul_kernel, n_w=n_w, n_rows=len(rows), n_tiles=len(tiles_p),
                             n_out=n_out, tn=tn, epilogue=epilogue)
    res = pl.pallas_call(
        kern,
        grid=(nj, m_p // tm),
        in_specs=in_specs,
        out_specs=out_specs,
        out_shape=out_shape,
        scratch_shapes=[pltpu.VMEM((k, n_w * tn), BF16)],
        compiler_params=_params(("arbitrary", "arbitrary")),
        name=name,
    )(a_p, a_s, *([w] * n_w), *rows, *tiles_p, *tiles_s)
    return res[:n_out], res[n_out:]


def _matmul_kernel(a_ref, w_ref, x_ref, o_ref):
    o_ref[...] = x_ref[...] + jnp.dot(a_ref[...], w_ref[...], preferred_element_type=F32)


def _column_tiles(w, tn):
    l, k, n = w.shape
    tn = min(tn, n)
    return w.astype(BF16).reshape(l, k, n // tn, tn).transpose(0, 2, 1, 3)


def _matmul_residual(a, w_tiles, layer, x, tm, name):
    m, k = a.shape
    _, nj, _, tn = w_tiles.shape
    tm = min(tm, m)
    assert m % tm == 0
    return pl.pallas_call(
        _matmul_kernel,
        grid=(m // tm, nj),
        in_specs=[pl.BlockSpec((tm, k), lambda i, j: (i, 0)),
                  pl.BlockSpec((None, None, k, tn), lambda i, j: (layer, j, 0, 0)),
                  pl.BlockSpec((tm, tn), lambda i, j: (i, j))],
        out_specs=pl.BlockSpec((tm, tn), lambda i, j: (i, j)),
        out_shape=jax.ShapeDtypeStruct((m, nj * tn), F32),
        compiler_params=_params(("arbitrary", "arbitrary")),
        name=name,
    )(a, w_tiles, x)


def _neg_abs(x):
    bits = lax.bitcast_convert_type(x, jnp.uint32) | jnp.uint32(0x80000000)
    return lax.bitcast_convert_type(bits, F32)


def _decay_exponents(g, p_ref, chunk):
    n = chunk // SUBLANES
    r8 = lax.broadcasted_iota(jnp.int32, (SUBLANES, HEAD_DIM), 0)
    tiles = [g[SUBLANES * j:SUBLANES * (j + 1)] for j in range(n)]
    pre = []
    for t in tiles:
        p = t
        d = 1
        while d < SUBLANES:
            p = p + jnp.where(r8 >= d, pltpu.roll(p, d, 0), 0.0)
            d *= 2
        pre.append(p)
    p_ref[...] = jnp.concatenate(pre, axis=0)

    def row(i):
        return jnp.broadcast_to(p_ref[i:i + 1, :], (SUBLANES, HEAD_DIM))

    tot = [row(SUBLANES * j + SUBLANES - 1) for j in range(n)]
    suf = [tot[j] - pre[j] for j in range(n)]
    end = [tot[0]]
    for j in range(1, n):
        end.append(end[j - 1] + tot[j])

    def upper(j, jm):
        return pre[j] if j - 1 == jm else pre[j] + (end[j - 1] - end[jm])

    def lower(j, jm):
        return suf[j] if j == jm else suf[j] + (end[jm] - end[j])

    levels = {1: jnp.concatenate([jnp.where((r8 & 1) == 1, t, 0.0) for t in tiles], axis=0)}
    m = 2
    while m < SUBLANES:
        out = []
        for j in range(n):
            base = SUBLANES * j
            mid = row(base + m - 1)
            for blk in range(1, SUBLANES // (2 * m)):
                mid = jnp.where(r8 < 2 * m * blk, mid, row(base + 2 * m * blk + m - 1))
            out.append(_neg_abs(pre[j] - mid))
        levels[m] = jnp.concatenate(out, axis=0)
        m *= 2
    while m < chunk:
        per = 2 * m // SUBLANES
        out = []
        for j in range(n):
            jm = (j // per) * per + per // 2 - 1
            out.append(upper(j, jm) if j > jm else lower(j, jm))
        levels[m] = jnp.concatenate(out, axis=0)
        m *= 2
    p_c = jnp.concatenate([pre[0]] + [pre[j] + end[j - 1] for j in range(1, n)], axis=0)
    x_c = jnp.concatenate([lower(j, n - 1) for j in range(n)], axis=0)
    return levels, p_c, x_c


def _hgrn_kernel(q_ref, k_ref, g_ref, v_ref, z_ref, s0_ref, gain_ref, o_ref, sfin_ref, st_ref,
                 p_ref, *, chunk, n_chunks, heads):
    l = pl.program_id(2)

    @pl.when(l == 0)
    def _():
        for hb in range(heads):
            st_ref[hb] = s0_ref[0, hb].T

    t_idx = lax.broadcasted_iota(jnp.int32, (chunk, chunk), 0)
    s_idx = lax.broadcasted_iota(jnp.int32, (chunk, chunk), 1)
    differ = t_idx ^ s_idx
    diagonal = t_idx == s_idx
    masks = []
    m = 1
    while m < chunk:
        masks.append((m, (t_idx > s_idx) & (differ >= m) & (differ < 2 * m)))
        m *= 2
    gain = gain_ref[...]
    nt_dims = (((1,), (1,)), ((), ()))
    tn_dims = (((0,), (0,)), ((), ()))

    def pair_rows(x0, x1):
        zero = jnp.zeros_like(x0)
        return jnp.concatenate([jnp.concatenate([x0, zero], axis=1),
                                jnp.concatenate([zero, x1], axis=1)], axis=0)

    def body(c, carry):
        r0 = pl.multiple_of(c * chunk, chunk)
        rows = pl.ds(r0, chunk)
        a_bf, q_in, k_out, decay, vs = [], [], [], [], []
        for hb in range(heads):
            q = q_ref[hb, rows, :]
            k = k_ref[hb, rows, :]
            levels, p_c, x_c = _decay_exponents(g_ref[hb, rows, :], p_ref.at[hb], chunk)

            a = jnp.where(diagonal, jnp.sum(q * k, axis=-1, keepdims=True), 0.0)
            for i in range(0, len(masks), 2):
                group = masks[i:i + 2]
                es = [jnp.exp(levels[m]) for m, _ in group]
                qs = [(q * e).astype(BF16) for e in es]
                ks = [(k * e).astype(BF16) for e in es]
                if len(group) == 2:
                    r = lax.dot_general(pair_rows(*qs), jnp.concatenate(ks, axis=1), nt_dims,
                                        preferred_element_type=F32)
                    parts = [r[:chunk], r[chunk:]]
                else:
                    parts = [lax.dot_general(qs[0], ks[0], nt_dims, preferred_element_type=F32)]
                for (_, mask), part in zip(group, parts):
                    a = jnp.where(mask, part, a)
            a_bf.append(a.astype(BF16))
            q_in.append((q * jnp.exp(p_c)).astype(BF16))
            k_out.append((k * jnp.exp(x_c)).astype(BF16))
            decay.append(jnp.exp(p_c[chunk - 1:chunk, :]))
            vs.append(v_ref[hb, rows, :])

        sts = [st_ref[hb] for hb in range(heads)]
        o_inter = []
        for hb in range(0, heads, 2):
            if hb + 1 < heads:
                r = lax.dot_general(pair_rows(q_in[hb], q_in[hb + 1]),
                                    jnp.concatenate([sts[hb].astype(BF16),
                                                     sts[hb + 1].astype(BF16)], axis=1),
                                    nt_dims, preferred_element_type=F32)
                o_inter += [r[:chunk], r[chunk:]]
            else:
                o_inter.append(lax.dot_general(q_in[hb], sts[hb].astype(BF16), nt_dims,
                                               preferred_element_type=F32))
        for hb in range(heads):
            lanes = slice(hb * HEAD_DIM, (hb + 1) * HEAD_DIM)
            o = o_inter[hb] + jnp.dot(a_bf[hb], vs[hb], preferred_element_type=F32)
            st_ref[hb] = decay[hb] * sts[hb] + lax.dot_general(vs[hb], k_out[hb], tn_dims,
                                                               preferred_element_type=F32)
            o = o * lax.rsqrt(jnp.mean(o * o, axis=-1, keepdims=True) + NORM_EPS) * gain
            o_ref[rows, lanes] = (o * z_ref[hb, rows, :]).astype(o_ref.dtype)
        return carry

    lax.fori_loop(0, n_chunks, body, 0)

    @pl.when(l == pl.num_programs(2) - 1)
    def _():
        for hb in range(heads):
            sfin_ref[0, hb] = st_ref[hb].T


def _hgrn_recurrence(q, k, g, v, z, s0, gain, n_streams, chunk, block_frames):
    h, m, _ = q.shape
    l = m // n_streams
    t = min(block_frames, l)
    hb = min(HEADS_PER_STEP, h)
    assert l % t == 0 and t % chunk == 0 and h % hb == 0
    nl = l // t
    seq = pl.BlockSpec((hb, t, HEAD_DIM), lambda bi, hi, li: (hi, bi * nl + li, 0))
    state = pl.BlockSpec((1, hb, HEAD_DIM, HEAD_DIM), lambda bi, hi, li: (bi, hi, 0, 0))
    kern = functools.partial(_hgrn_kernel, chunk=chunk, n_chunks=t // chunk, heads=hb)
    return pl.pallas_call(
        kern,
        grid=(n_streams, h // hb, nl),
        in_specs=[seq, seq, seq, seq, seq, state,
                  pl.BlockSpec((1, HEAD_DIM), lambda bi, hi, li: (0, 0))],
        out_specs=[pl.BlockSpec((t, hb * HEAD_DIM), lambda bi, hi, li: (bi * nl + li, hi)), state],
        out_shape=[jax.ShapeDtypeStruct((m, h * HEAD_DIM), BF16),
                   jax.ShapeDtypeStruct(s0.shape, s0.dtype)],
        scratch_shapes=[pltpu.VMEM((hb, HEAD_DIM, HEAD_DIM), F32),
                        pltpu.VMEM((hb, chunk, HEAD_DIM), F32)],
        compiler_params=_params(("arbitrary", "arbitrary", "arbitrary")),
        name="hgrn_recurrence",
    )(q, k, g, v, z, s0, gain.reshape(1, HEAD_DIM).astype(F32))


def _conv_kernel(gb_ref, cu_ref, buf_ref, w_ref, o_ref, state_ref, carry_ref):
    l = pl.program_id(2)

    @pl.when(l == 0)
    def _():
        carry_ref[...] = buf_ref[0]

    cu = cu_ref[0]
    tl = cu.shape[0]
    row = lax.broadcasted_iota(jnp.int32, cu.shape, 0)
    prev2 = carry_ref[0:1, :]
    prev1 = carry_ref[1:2, :]
    c1 = jnp.where(row == 0, prev1, pltpu.roll(cu, 1, 0))
    c2 = jnp.where(row == 0, prev2, jnp.where(row == 1, prev1, pltpu.roll(cu, 2, 0)))
    conv = w_ref[0:1, :] * c2 + w_ref[1:2, :] * c1 + w_ref[2:3, :] * cu
    o_ref[0] = (gb_ref[0] * conv).astype(o_ref.dtype)
    last = cu[tl - 2:tl, :]
    carry_ref[...] = last
    state_ref[0] = last


def _short_conv(gb, cu, buf, w, block_frames, td=512):
    b, l, d = cu.shape
    tl = min(block_frames, l)
    td = min(td, d)
    assert l % tl == 0 and d % td == 0 and tl >= 2
    seq = pl.BlockSpec((1, tl, td), lambda bi, di, li: (bi, li, di))
    two = pl.BlockSpec((1, 2, td), lambda bi, di, li: (bi, 0, di))
    return pl.pallas_call(
        _conv_kernel,
        grid=(b, d // td, l // tl),
        in_specs=[seq, seq, two, pl.BlockSpec((3, td), lambda bi, di, li: (0, di))],
        out_specs=[seq, two],
        out_shape=[jax.ShapeDtypeStruct((b, l, d), BF16),
                   jax.ShapeDtypeStruct(buf.shape, buf.dtype)],
        scratch_shapes=[pltpu.VMEM((2, td), F32)],
        compiler_params=_params(("arbitrary", "arbitrary", "arbitrary")),
        name="short_conv",
    )(gb, cu, buf, w)


def _ffn(x_p, x_s, norm_gain, w_in, w_out_tiles, layer, d_ff):
    h_p = _rmsnorm(x_p, norm_gain, BF16)
    h_s = _rmsnorm(x_s, norm_gain, BF16)
    (act_p,), (act_s,) = _ws_matmul(h_p, h_s, w_in, layer, (0, d_ff), d_ff, [], [], [], [BF16],
                                    _swiglu_epilogue, tm=1024, tn=256, name="ffn_in")
    x_p = _matmul_residual(act_p, w_out_tiles, layer, x_p, tm=512, name="ffn_out")
    x_s = _matmul_residual(act_s, w_out_tiles, layer, x_s, tm=512, name="ffn_out")
    return x_p, x_s


def _mixer_out(a_p, a_s, w, x_p, x_s, name):
    w_tiles = _column_tiles(w, 512)
    x_p = _matmul_residual(a_p, w_tiles, 0, x_p, tm=1024, name=name)
    x_s = _matmul_residual(a_s, w_tiles, 0, x_s, tm=1024, name=name)
    return x_p, x_s


def kernel(x_prompt, x_sample, state_hgrn, state_conv, hgrn_lb_logits, hgrn_w_in, hgrn_out_gain,
           hgrn_w_out, conv_w_in, conv_w, conv_w_out, norm_mix, norm_ffn, ffn_w_in, ffn_w_out,
           norm_final):
    bp, lp, d = x_prompt.shape
    bs, ls, _ = x_sample.shape
    d_ff = ffn_w_out.shape[1]
    x_p = x_prompt.reshape(bp * lp, d)
    x_s = x_sample.reshape(bs * ls, d)
    ffn_w_out_tiles = _column_tiles(ffn_w_out, 256)
    logits = hgrn_lb_logits.astype(F32)

    h_p = _rmsnorm(x_p, norm_mix[0], BF16)
    h_s = _rmsnorm(x_s, norm_mix[0], BF16)
    proj_p, proj_s = _ws_matmul(
        h_p, h_s, hgrn_w_in, 0, (0, d, 2 * d, 3 * d), d, [logits], [], [],
        [F32, F32, F32, BF16, F32], functools.partial(_hgrn_proj_epilogue, layer=0),
        tm=1024, tn=HEAD_DIM, name="hgrn_proj", head_major=True)
    zero_state = jnp.zeros((bp,) + state_hgrn.shape[2:], state_hgrn.dtype)
    o_p, hgrn_p = _hgrn_recurrence(*proj_p, zero_state, hgrn_out_gain[0], bp, PROMPT_CHUNK,
                                   block_frames=1024)
    o_s, hgrn_s = _hgrn_recurrence(*proj_s, state_hgrn[0], hgrn_out_gain[0], bs, ls,
                                   block_frames=ls)
    x_p, x_s = _mixer_out(o_p, o_s, hgrn_w_out, x_p, x_s, "hgrn_out")
    x_p, x_s = _ffn(x_p, x_s, norm_ffn[0], ffn_w_in, ffn_w_out_tiles, 0, d_ff)

    h_p = _rmsnorm(x_p, norm_mix[1], BF16)
    h_s = _rmsnorm(x_s, norm_mix[1], BF16)
    (gb_p, cu_p), (gb_s, cu_s) = _ws_matmul(
        h_p, h_s, conv_w_in, 0, (0, d, 2 * d), d, [], [], [], [F32, F32], _conv_in_epilogue,
        tm=512, tn=256, name="conv_in")
    w_conv = conv_w[0].astype(F32)
    zero_buf = jnp.zeros((bp,) + state_conv.shape[2:], state_conv.dtype)
    gated_p, conv_p = _short_conv(gb_p.reshape(bp, lp, d), cu_p.reshape(bp, lp, d), zero_buf,
                                  w_conv, block_frames=1024)
    gated_s, conv_s = _short_conv(gb_s.reshape(bs, ls, d), cu_s.reshape(bs, ls, d), state_conv[0],
                                  w_conv, block_frames=ls)
    x_p, x_s = _mixer_out(gated_p.reshape(bp * lp, d), gated_s.reshape(bs * ls, d),
                          conv_w_out, x_p, x_s, "conv_out")
    x_p, x_s = _ffn(x_p, x_s, norm_ffn[1], ffn_w_in, ffn_w_out_tiles, 1, d_ff)

    y_p = _rmsnorm(x_p, norm_final, F32).reshape(bp, lp, d)
    y_s = _rmsnorm(x_s, norm_final, F32).reshape(bs, ls, d)
    return (y_p, y_s, hgrn_p[None], hgrn_s[None], conv_p[None], conv_s[None])
```

```python
import functools

import jax
import jax.numpy as jnp
from jax import lax
from jax.experimental import pallas as pl
from jax.experimental.pallas import tpu as pltpu

F32 = jnp.float32
BF16 = jnp.bfloat16

NORM_EPS = 1e-6
HEAD_DIM = 128
PROMPT_CHUNK = 64
SUBLANES = 8
HEADS_PER_STEP = 4
ROW_SUBBLOCK = 256
VMEM_LIMIT_BYTES = 56 * 1024 * 1024
DIRECT_DECAY_MIN_LOG = -60.0


def _params(semantics):
    return pltpu.CompilerParams(dimension_semantics=semantics,
                                vmem_limit_bytes=VMEM_LIMIT_BYTES)


def _rmsnorm_kernel(x_ref, g_ref, o_ref):
    x = x_ref[...]
    y = x * lax.rsqrt(jnp.mean(x * x, axis=-1, keepdims=True) + NORM_EPS)
    o_ref[...] = (y * g_ref[...]).astype(o_ref.dtype)


def _rmsnorm(x, gain, out_dtype):
    m, d = x.shape
    tm = min(256, m)
    return pl.pallas_call(
        _rmsnorm_kernel,
        grid=(m // tm,),
        in_specs=[pl.BlockSpec((tm, d), lambda i: (i, 0)),
                  pl.BlockSpec((1, d), lambda i: (0, 0))],
        out_specs=pl.BlockSpec((tm, d), lambda i: (i, 0)),
        out_shape=jax.ShapeDtypeStruct((m, d), out_dtype),
        compiler_params=_params(("arbitrary",)),
        name="rmsnorm",
    )(x, gain.reshape(1, d).astype(F32))


def _ws_matmul_kernel(*refs, n_w, n_rows, n_ex_p, n_ex_s, n_out_p, n_out_s, tn, body_p, body_s):
    it = iter(refs)
    take = lambda n: [next(it) for _ in range(n)]
    (a_p, a_s), w_refs, row_refs = take(2), take(n_w), take(n_rows)
    ex_p, ex_s, outs_p, outs_s = take(n_ex_p), take(n_ex_s), take(n_out_p), take(n_out_s)
    wcat, *scratch = list(it)
    first = pl.program_id(1) == 0

    def products(a_ref):
        def zs(rs):
            z = jnp.dot(a_ref[rs, :], wcat[...], preferred_element_type=F32)
            return [z[:, c * tn:(c + 1) * tn] for c in range(n_w)]
        return zs

    @pl.when(first)
    def _():
        for c, w_ref in enumerate(w_refs):
            wcat[:, c * tn:(c + 1) * tn] = w_ref[...].astype(BF16)
        body_s(products(a_s), a_s.shape[0], row_refs, ex_s, outs_s, scratch, None)

    body_p(products(a_p), a_p.shape[0], row_refs, ex_p, outs_p, scratch, first)


def _ws_matmul(a_p, a_s, w, layer, col_offsets, tn, rows, extras_p, extras_s, outs_p, outs_s,
               body_p, body_s, tm, name, scratch=()):
    m_p, k = a_p.shape
    m_s = a_s.shape[0]
    tm = min(tm, m_p)
    n_w = len(col_offsets)
    assert m_p % tm == 0 and all(c % tn == 0 for c in col_offsets)
    nj = rows[0].shape[1] // tn if rows else outs_p[0][0][-1] // tn
    in_specs = [pl.BlockSpec((tm, k), lambda j, i: (i, 0)),
                pl.BlockSpec((m_s, k), lambda j, i: (0, 0))]
    for c in col_offsets:
        in_specs.append(pl.BlockSpec((None, k, tn), lambda j, i, c=c: (layer, 0, c // tn + j)))
    for r in rows:
        in_specs.append(pl.BlockSpec((r.shape[0], tn), lambda j, i: (0, j)))
    in_specs += [pl.BlockSpec(blk, imap) for _, blk, imap in list(extras_p) + list(extras_s)]
    outs = list(outs_p) + list(outs_s)
    kern = functools.partial(
        _ws_matmul_kernel, n_w=n_w, n_rows=len(rows), n_ex_p=len(extras_p), n_ex_s=len(extras_s),
        n_out_p=len(outs_p), n_out_s=len(outs_s), tn=tn, body_p=body_p, body_s=body_s)
    res = pl.pallas_call(
        kern,
        grid=(nj, m_p // tm),
        in_specs=in_specs,
        out_specs=[pl.BlockSpec(blk, imap) for _, _, blk, imap in outs],
        out_shape=[jax.ShapeDtypeStruct(shape, dt) for shape, dt, _, _ in outs],
        scratch_shapes=[pltpu.VMEM((k, n_w * tn), BF16)] + list(scratch),
        compiler_params=_params(("arbitrary", "arbitrary")),
        name=name,
    )(a_p, a_s, *([w] * n_w), *rows, *[x for x, _, _ in extras_p], *[x for x, _, _ in extras_s])
    return res[:len(outs_p)], res[len(outs_p):]


def _rowwise_body(epilogue):
    def body(zs, m, row_refs, extra_refs, out_refs, scratch, first):
        sub = min(ROW_SUBBLOCK, m)
        row_vals = [r[...] for r in row_refs]
        for r0 in range(0, m, sub):
            for o_ref, r in zip(out_refs, epilogue(zs(slice(r0, r0 + sub)), row_vals)):
                n = r.shape[0]
                lo = r0 * n // sub
                o_ref[lo:lo + n, :] = r.astype(o_ref.dtype)
    return body


def _row_major(m, tm, n_cols, tn, dtype, sample):
    imap = (lambda j, i: (0, j)) if sample else (lambda j, i: (i, j))
    return ((m, n_cols), dtype, (tm, tn), imap)


def _head_major(m, tm, n_heads, tn, dtype, sample, group=1):
    imap = (lambda j, i: (j, 0, 0)) if sample else (lambda j, i: (j, i, 0))
    return ((n_heads, m // group, tn), dtype, (None, tm // group, tn), imap)


def _silu(z):
    return z * jax.nn.sigmoid(z)


def _hgrn_proj_epilogue(zs, row_vals, *, layer, chunk):
    zq, zf, zi, zg = zs
    logits = row_vals[0]
    e = jnp.exp(logits - jnp.max(logits, axis=0, keepdims=True))
    lb = jnp.sum(e[:layer + 1], axis=0, keepdims=True) / jnp.sum(e, axis=0, keepdims=True)
    f = lb + (1.0 - lb) * jax.nn.sigmoid(zf)
    k = (1.0 - lb) * jax.nn.sigmoid(-zf)
    g = jnp.log(f)
    chunk_sums = jnp.sum(g.reshape(g.shape[0] // chunk, chunk, g.shape[1]), axis=1)
    return [_silu(zq), k, g, zi, _silu(zg), chunk_sums]


def _swiglu_epilogue(zs, row_vals):
    return [_silu(zs[0]) * zs[1]]


def _conv_rows(cu, prev2, prev1, w):
    row = lax.broadcasted_iota(jnp.int32, cu.shape, 0)
    c1 = jnp.where(row == 0, prev1, pltpu.roll(cu, 1, 0))
    c2 = jnp.where(row == 0, prev2, jnp.where(row == 1, prev1, pltpu.roll(cu, 2, 0)))
    return w[0:1, :] * c2 + w[1:2, :] * c1 + w[2:3, :] * cu


def _conv_body_prompt(zs, m, row_refs, extra_refs, out_refs, scratch, first):
    (buf_ref,), (gated_ref, state_ref), (carry_ref,) = extra_refs, out_refs, scratch
    w = row_refs[0][...]

    @pl.when(first)
    def _():
        carry_ref[...] = buf_ref[...]

    prev2, prev1 = carry_ref[0:1, :], carry_ref[1:2, :]
    sub = min(ROW_SUBBLOCK, m)
    for r0 in range(0, m, sub):
        gb, gc, u = zs(slice(r0, r0 + sub))
        cu = gc * u
        gated_ref[r0:r0 + sub, :] = (gb * _conv_rows(cu, prev2, prev1, w)).astype(gated_ref.dtype)
        prev2, prev1 = cu[sub - 2:sub - 1, :], cu[sub - 1:sub, :]
    last = jnp.concatenate([prev2, prev1], axis=0)
    carry_ref[...] = last
    state_ref[...] = last


def _conv_body_sample(zs, m, row_refs, extra_refs, out_refs, scratch, first, *, frames):
    (buf_ref,), (gated_ref, state_ref) = extra_refs, out_refs
    w = row_refs[0][...]
    gb, gc, u = zs(slice(0, m))
    cu = gc * u
    for b in range(m // frames):
        rs = slice(b * frames, (b + 1) * frames)
        cub = cu[rs, :]
        buf = buf_ref[b]
        conv = _conv_rows(cub, buf[0:1, :], buf[1:2, :], w)
        gated_ref[rs, :] = (gb[rs, :] * conv).astype(gated_ref.dtype)
        state_ref[b] = cub[frames - 2:frames, :]


def _matmul_kernel(a_ref, w_ref, x_ref, o_ref):
    o_ref[...] = x_ref[...] + jnp.dot(a_ref[...], w_ref[...], preferred_element_type=F32)


def _matmul_residual(a, w, layer, x, tm, tn, k_splits, name):
    m, k = a.shape
    n = w.shape[2]
    tm = min(tm, m)
    tn = min(tn, n)
    tk = k // k_splits
    assert m % tm == 0 and n % tn == 0 and k % k_splits == 0
    for ks in range(k_splits):
        x = pl.pallas_call(
            _matmul_kernel,
            grid=(m // tm, n // tn),
            in_specs=[pl.BlockSpec((tm, tk), lambda i, j, ks=ks: (i, ks)),
                      pl.BlockSpec((None, tk, tn), lambda i, j, ks=ks: (layer, ks, j)),
                      pl.BlockSpec((tm, tn), lambda i, j: (i, j))],
            out_specs=pl.BlockSpec((tm, tn), lambda i, j: (i, j)),
            out_shape=jax.ShapeDtypeStruct((m, n), F32),
            compiler_params=_params(("arbitrary", "arbitrary")),
            name=name,
        )(a, w, x)
    return x


def _neg_abs(x):
    bits = lax.bitcast_convert_type(x, jnp.uint32) | jnp.uint32(0x80000000)
    return lax.bitcast_convert_type(bits, F32)


class _ChunkSums:
    def __init__(self, g, p_ref, chunk):
        self.chunk, self.n, self.p_ref = chunk, chunk // SUBLANES, p_ref
        n = self.n
        self.r8 = lax.broadcasted_iota(jnp.int32, (SUBLANES, HEAD_DIM), 0)
        self.tiles = [g[SUBLANES * j:SUBLANES * (j + 1)] for j in range(n)]
        self.pre = []
        for t in self.tiles:
            p = t
            d = 1
            while d < SUBLANES:
                p = p + jnp.where(self.r8 >= d, pltpu.roll(p, d, 0), 0.0)
                d *= 2
            self.pre.append(p)
        p_ref[...] = jnp.concatenate(self.pre, axis=0)
        tot = [self.row(SUBLANES * j + SUBLANES - 1) for j in range(n)]
        self.suf = [tot[j] - self.pre[j] for j in range(n)]
        self.end = [tot[0]]
        for j in range(1, n):
            self.end.append(self.end[j - 1] + tot[j])

    def row(self, i):
        return jnp.broadcast_to(self.p_ref[i:i + 1, :], (SUBLANES, HEAD_DIM))

    def upper(self, j, jm):
        return self.pre[j] if j - 1 == jm else self.pre[j] + (self.end[j - 1] - self.end[jm])

    def lower(self, j, jm):
        return self.suf[j] if j == jm else self.suf[j] + (self.end[jm] - self.end[j])

    def from_start(self):
        return jnp.concatenate([self.pre[j] + self.end[j - 1] if j else self.pre[0]
                                for j in range(self.n)], axis=0)

    def to_end(self):
        return jnp.concatenate([self.lower(j, self.n - 1) for j in range(self.n)], axis=0)

    def levels(self):
        r8, n = self.r8, self.n
        out = {1: jnp.concatenate([jnp.where((r8 & 1) == 1, t, 0.0) for t in self.tiles], axis=0)}
        m = 2
        while m < SUBLANES:
            level = []
            for j in range(n):
                base = SUBLANES * j
                mid = self.row(base + m - 1)
                for blk in range(1, SUBLANES // (2 * m)):
                    mid = jnp.where(r8 < 2 * m * blk, mid, self.row(base + 2 * m * blk + m - 1))
                level.append(_neg_abs(self.pre[j] - mid))
            out[m] = jnp.concatenate(level, axis=0)
            m *= 2
        while m < self.chunk:
            per = 2 * m // SUBLANES
            level = []
            for j in range(n):
                jm = (j // per) * per + per // 2 - 1
                level.append(self.upper(j, jm) if j > jm else self.lower(j, jm))
            out[m] = jnp.concatenate(level, axis=0)
            m *= 2
        return out


def _hgrn_kernel(direct_ref, q_ref, k_ref, g_ref, v_ref, z_ref, s0_ref, gain_ref, o_ref, sfin_ref,
                 st_ref, p_ref, *, chunk, n_chunks, heads):
    bi, hi, li = pl.program_id(0), pl.program_id(1), pl.program_id(2)
    direct = direct_ref[(bi * pl.num_programs(2) + li) * pl.num_programs(1) + hi] == 1

    @pl.when(li == 0)
    def _():
        for hb in range(heads):
            st_ref[hb] = s0_ref[0, hb].T

    t_idx = lax.broadcasted_iota(jnp.int32, (chunk, chunk), 0)
    s_idx = lax.broadcasted_iota(jnp.int32, (chunk, chunk), 1)
    differ = t_idx ^ s_idx
    masks = []
    m = 1
    while m < chunk:
        masks.append((m, (t_idx > s_idx) & (differ >= m) & (differ < 2 * m)))
        m *= 2
    gain = gain_ref[...]
    nt_dims = (((1,), (1,)), ((), ()))
    tn_dims = (((0,), (0,)), ((), ()))

    def pair_rows(x0, x1):
        zero = jnp.zeros_like(x0)
        return jnp.concatenate([jnp.concatenate([x0, zero], axis=1),
                                jnp.concatenate([zero, x1], axis=1)], axis=0)

    def paired_nt(lhs, rhs):
        out = []
        for i in range(0, len(lhs), 2):
            if i + 1 < len(lhs):
                r = lax.dot_general(pair_rows(lhs[i], lhs[i + 1]),
                                    jnp.concatenate([rhs[i], rhs[i + 1]], axis=1), nt_dims,
                                    preferred_element_type=F32)
                half = lhs[i].shape[0]
                out += [r[:half], r[half:]]
            else:
                out.append(lax.dot_general(lhs[i], rhs[i], nt_dims, preferred_element_type=F32))
        return out

    def finish(rows, a_bf, q_in, k_out, decay):
        sts = [st_ref[hb] for hb in range(heads)]
        o_inter = paired_nt(q_in, [st.astype(BF16) for st in sts])
        for hb in range(heads):
            lanes = slice(hb * HEAD_DIM, (hb + 1) * HEAD_DIM)
            v = v_ref[hb, rows, :]
            o = o_inter[hb] + jnp.dot(a_bf[hb], v, preferred_element_type=F32)
            st_ref[hb] = decay[hb] * sts[hb] + lax.dot_general(v, k_out[hb], tn_dims,
                                                               preferred_element_type=F32)
            o = o * lax.rsqrt(jnp.mean(o * o, axis=-1, keepdims=True) + NORM_EPS) * gain
            o_ref[rows, lanes] = (o * z_ref[hb, rows, :]).astype(o_ref.dtype)

    def chunk_rows(c):
        return pl.ds(pl.multiple_of(c * chunk, chunk), chunk)

    def level_body(c, carry):
        rows = chunk_rows(c)
        a_bf, q_in, k_out, decay = [], [], [], []
        for hb in range(heads):
            q = q_ref[hb, rows, :]
            k = k_ref[hb, rows, :]
            sums = _ChunkSums(g_ref[hb, rows, :], p_ref.at[hb], chunk)
            levels, p_c = sums.levels(), sums.from_start()
            a = jnp.where(t_idx == s_idx, jnp.sum(q * k, axis=-1, keepdims=True), 0.0)
            es = [jnp.exp(levels[m]) for m, _ in masks]
            parts = paired_nt([(q * e).astype(BF16) for e in es], [(k * e).astype(BF16) for e in es])
            for (_, mask), part in zip(masks, parts):
                a = jnp.where(mask, part, a)
            a_bf.append(a.astype(BF16))
            q_in.append((q * jnp.exp(p_c)).astype(BF16))
            k_out.append((k * jnp.exp(sums.to_end())).astype(BF16))
            decay.append(jnp.exp(p_c[chunk - 1:chunk, :]))
        finish(rows, a_bf, q_in, k_out, decay)
        return carry

    def direct_body(c, carry):
        rows = chunk_rows(c)
        q_in, k_neg, k_out, decay = [], [], [], []
        for hb in range(heads):
            q = q_ref[hb, rows, :]
            k = k_ref[hb, rows, :]
            sums = _ChunkSums(g_ref[hb, rows, :], p_ref.at[hb], chunk)
            p_c = sums.from_start()
            q_in.append((q * jnp.exp(p_c)).astype(BF16))
            k_neg.append((k * jnp.exp(-p_c)).astype(BF16))
            k_out.append((k * jnp.exp(sums.to_end())).astype(BF16))
            decay.append(jnp.exp(p_c[chunk - 1:chunk, :]))
        a_bf = [jnp.where(t_idx >= s_idx, a, 0.0).astype(BF16) for a in paired_nt(q_in, k_neg)]
        finish(rows, a_bf, q_in, k_out, decay)
        return carry

    @pl.when(direct)
    def _():
        lax.fori_loop(0, n_chunks, direct_body, 0, unroll=min(4, n_chunks))

    @pl.when(jnp.logical_not(direct))
    def _():
        lax.fori_loop(0, n_chunks, level_body, 0)

    @pl.when(li == pl.num_programs(2) - 1)
    def _():
        for hb in range(heads):
            sfin_ref[0, hb] = st_ref[hb].T


def _hgrn_recurrence(q, k, g, v, z, chunk_sums, s0, gain, n_streams, chunk, block_frames):
    h, m, _ = q.shape
    l = m // n_streams
    t = min(block_frames, l)
    hb = min(HEADS_PER_STEP, h)
    assert l % t == 0 and t % chunk == 0 and h % hb == 0
    nl = l // t
    least = jnp.min(chunk_sums.reshape(h // hb, hb, n_streams * nl, t // chunk, HEAD_DIM),
                    axis=(1, 3, 4))
    direct = (least.T >= DIRECT_DECAY_MIN_LOG).astype(jnp.int32).reshape(-1)
    seq = pl.BlockSpec((hb, t, HEAD_DIM), lambda bi, hi, li, flags: (hi, bi * nl + li, 0))
    state = pl.BlockSpec((1, hb, HEAD_DIM, HEAD_DIM), lambda bi, hi, li, flags: (bi, hi, 0, 0))
    kern = functools.partial(_hgrn_kernel, chunk=chunk, n_chunks=t // chunk, heads=hb)
    return pl.pallas_call(
        kern,
        grid_spec=pltpu.PrefetchScalarGridSpec(
            num_scalar_prefetch=1,
            grid=(n_streams, h // hb, nl),
            in_specs=[seq, seq, seq, seq, seq, state,
                      pl.BlockSpec((1, HEAD_DIM), lambda bi, hi, li, flags: (0, 0))],
            out_specs=[pl.BlockSpec((t, hb * HEAD_DIM),
                                    lambda bi, hi, li, flags: (bi * nl + li, hi)), state],
            scratch_shapes=[pltpu.VMEM((hb, HEAD_DIM, HEAD_DIM), F32),
                            pltpu.VMEM((hb, chunk, HEAD_DIM), F32)]),
        out_shape=[jax.ShapeDtypeStruct((m, h * HEAD_DIM), BF16),
                   jax.ShapeDtypeStruct(s0.shape, s0.dtype)],
        compiler_params=_params(("arbitrary", "arbitrary", "arbitrary")),
        name="hgrn_recurrence",
    )(direct, q, k, g, v, z, s0, gain.reshape(1, HEAD_DIM).astype(F32))


def _ffn(x_p, x_s, norm_gain, w_in, w_out_bf16, layer, d_ff):
    h_p = _rmsnorm(x_p, norm_gain, BF16)
    h_s = _rmsnorm(x_s, norm_gain, BF16)
    m_p, m_s = x_p.shape[0], x_s.shape[0]
    tm, tn = min(1024, m_p), 256
    body = _rowwise_body(_swiglu_epilogue)
    (act_p,), (act_s,) = _ws_matmul(
        h_p, h_s, w_in, layer, (0, d_ff), tn, [], [], [],
        [_row_major(m_p, tm, d_ff, tn, BF16, False)], [_row_major(m_s, m_s, d_ff, tn, BF16, True)],
        body, body, tm, "ffn_in")
    x_p = _matmul_residual(act_p, w_out_bf16, layer, x_p, 1024, 512, 2, "ffn_out")
    x_s = _matmul_residual(act_s, w_out_bf16, layer, x_s, 1024, 512, 2, "ffn_out")
    return x_p, x_s


def _mixer_out(a_p, a_s, w, x_p, x_s, name):
    w_bf16 = w.astype(BF16)
    x_p = _matmul_residual(a_p, w_bf16, 0, x_p, 1024, 512, 1, name)
    x_s = _matmul_residual(a_s, w_bf16, 0, x_s, 1024, 512, 1, name)
    return x_p, x_s


def kernel(x_prompt, x_sample, state_hgrn, state_conv, hgrn_lb_logits, hgrn_w_in, hgrn_out_gain,
           hgrn_w_out, conv_w_in, conv_w, conv_w_out, norm_mix, norm_ffn, ffn_w_in, ffn_w_out,
           norm_final):
    bp, lp, d = x_prompt.shape
    bs, ls, _ = x_sample.shape
    assert bp == 1
    m_p, m_s = bp * lp, bs * ls
    d_ff = ffn_w_out.shape[1]
    n_heads = d // HEAD_DIM
    x_p = x_prompt.reshape(m_p, d)
    x_s = x_sample.reshape(m_s, d)
    ffn_w_out_bf16 = ffn_w_out.astype(BF16)

    h_p = _rmsnorm(x_p, norm_mix[0], BF16)
    h_s = _rmsnorm(x_s, norm_mix[0], BF16)
    tm = min(1024, m_p)
    proj_dtypes = [F32, F32, F32, BF16, F32]
    head_outs = lambda m, t, chunk, sample: (
        [_head_major(m, t, n_heads, HEAD_DIM, dt, sample) for dt in proj_dtypes]
        + [_head_major(m, t, n_heads, HEAD_DIM, F32, sample, group=chunk)])
    epilogue = lambda chunk: _rowwise_body(
        functools.partial(_hgrn_proj_epilogue, layer=0, chunk=chunk))
    proj_p, proj_s = _ws_matmul(
        h_p, h_s, hgrn_w_in, 0, (0, d, 2 * d, 3 * d), HEAD_DIM, [hgrn_lb_logits.astype(F32)], [], [],
        head_outs(m_p, tm, PROMPT_CHUNK, False), head_outs(m_s, m_s, ls, True),
        epilogue(PROMPT_CHUNK), epilogue(ls), tm, "hgrn_proj")
    zero_state = jnp.zeros((bp,) + state_hgrn.shape[2:], state_hgrn.dtype)
    o_p, hgrn_p = _hgrn_recurrence(*proj_p, zero_state, hgrn_out_gain[0], bp, PROMPT_CHUNK,
                                   block_frames=1024)
    o_s, hgrn_s = _hgrn_recurrence(*proj_s, state_hgrn[0], hgrn_out_gain[0], bs, ls,
                                   block_frames=ls)
    x_p, x_s = _mixer_out(o_p, o_s, hgrn_w_out, x_p, x_s, "hgrn_out")
    x_p, x_s = _ffn(x_p, x_s, norm_ffn[0], ffn_w_in, ffn_w_out_bf16, 0, d_ff)

    h_p = _rmsnorm(x_p, norm_mix[1], BF16)
    h_s = _rmsnorm(x_s, norm_mix[1], BF16)
    tm, tn = min(512, m_p), 256
    two_rows = lambda b: ((b, 2, tn), lambda j, i: (0, 0, j))
    zero_buf = jnp.zeros((bp,) + state_conv.shape[2:], state_conv.dtype)
    (gated_p, conv_p), (gated_s, conv_s) = _ws_matmul(
        h_p, h_s, conv_w_in, 0, (0, d, 2 * d), tn, [conv_w[0].astype(F32)],
        [(zero_buf, (None, 2, tn), two_rows(None)[1])], [(state_conv[0],) + two_rows(bs)],
        [_row_major(m_p, tm, d, tn, BF16, False),
         ((bp, 2, d), state_conv.dtype, (None, 2, tn), two_rows(None)[1])],
        [_row_major(m_s, m_s, d, tn, BF16, True), ((bs, 2, d), state_conv.dtype) + two_rows(bs)],
        _conv_body_prompt, functools.partial(_conv_body_sample, frames=ls), tm, "conv_in",
        scratch=[pltpu.VMEM((2, tn), F32)])
    x_p, x_s = _mixer_out(gated_p, gated_s, conv_w_out, x_p, x_s, "conv_out")
    x_p, x_s = _ffn(x_p, x_s, norm_ffn[1], ffn_w_in, ffn_w_out_bf16, 1, d_ff)

    y_p = _rmsnorm(x_p, norm_final, F32).reshape(bp, lp, d)
    y_s = _rmsnorm(x_s, norm_final, F32).reshape(bs, ls, d)
    return (y_p, y_s, hgrn_p[None], hgrn_s[None], conv_p[None], conv_s[None])
```

```python
import functools

import jax
import jax.numpy as jnp
from jax import lax
from jax.experimental import pallas as pl
from jax.experimental.pallas import tpu as pltpu

F32 = jnp.float32
BF16 = jnp.bfloat16

NORM_EPS = 1e-6
HEAD_DIM = 128
PROMPT_CHUNK = 64
SUBLANES = 8
HEADS_PER_STEP = 4
ROW_SUBBLOCK = 256
VMEM_LIMIT_BYTES = 56 * 1024 * 1024
DIRECT_DECAY_MIN_LOG = -60.0


def _params(semantics):
    return pltpu.CompilerParams(dimension_semantics=semantics,
                                vmem_limit_bytes=VMEM_LIMIT_BYTES)


def _rmsnorm_kernel(x_ref, g_ref, o_ref):
    x = x_ref[...]
    y = x * lax.rsqrt(jnp.mean(x * x, axis=-1, keepdims=True) + NORM_EPS)
    o_ref[...] = (y * g_ref[...]).astype(o_ref.dtype)


def _rmsnorm(x, gain, out_dtype):
    m, d = x.shape
    tm = min(256, m)
    return pl.pallas_call(
        _rmsnorm_kernel,
        grid=(m // tm,),
        in_specs=[pl.BlockSpec((tm, d), lambda i: (i, 0)),
                  pl.BlockSpec((1, d), lambda i: (0, 0))],
        out_specs=pl.BlockSpec((tm, d), lambda i: (i, 0)),
        out_shape=jax.ShapeDtypeStruct((m, d), out_dtype),
        compiler_params=_params(("arbitrary",)),
        name="rmsnorm",
    )(x, gain.reshape(1, d).astype(F32))


def _ws_matmul_kernel(*refs, n_a, n_w, n_rows, n_ex_p, n_ex_s, n_out_p, n_out_s, tn, body_p,
                      body_s):
    it = iter(refs)
    take = lambda n: [next(it) for _ in range(n)]
    a_p, (a_s,), w_refs, row_refs = take(n_a), take(1), take(n_w), take(n_rows)
    ex_p, ex_s, outs_p, outs_s = take(n_ex_p), take(n_ex_s), take(n_out_p), take(n_out_s)
    wcat, *scratch = list(it)
    first = pl.program_id(1) == 0

    def products(a_refs):
        per = a_refs[0].shape[0]

        def zs(rs):
            part = rs.start // per
            a = a_refs[part][rs.start - part * per:rs.stop - part * per, :]
            z = jnp.dot(a, wcat[...], preferred_element_type=F32)
            return [z[:, c * tn:(c + 1) * tn] for c in range(n_w)]
        return zs

    @pl.when(first)
    def _():
        for c, w_ref in enumerate(w_refs):
            wcat[:, c * tn:(c + 1) * tn] = w_ref[...].astype(BF16)
        body_s(products([a_s]), a_s.shape[0], row_refs, ex_s, outs_s, scratch, None)

    body_p(products(a_p), len(a_p) * a_p[0].shape[0], row_refs, ex_p, outs_p, scratch, first)


def _ws_matmul(a_p, a_s, w, layer, col_offsets, tn, rows, extras_p, extras_s, outs_p, outs_s,
               body_p, body_s, tm, name, scratch=(), a_split=1, a_buffers=2):
    m_p, k = a_p.shape
    m_s = a_s.shape[0]
    tm = min(tm, m_p)
    n_w = len(col_offsets)
    assert m_p % tm == 0 and all(c % tn == 0 for c in col_offsets)
    nj = rows[0].shape[1] // tn if rows else outs_p[0][0][-1] // tn
    if tm % (a_split * ROW_SUBBLOCK):
        a_split = 1
    in_specs = [pl.BlockSpec((tm // a_split, k), lambda j, i, s=s: (i * a_split + s, 0),
                             pipeline_mode=pl.Buffered(a_buffers)) for s in range(a_split)]
    in_specs.append(pl.BlockSpec((m_s, k), lambda j, i: (0, 0), pipeline_mode=pl.Buffered(1)))
    for c in col_offsets:
        in_specs.append(pl.BlockSpec((None, k, tn), lambda j, i, c=c: (layer, 0, c // tn + j)))
    for r in rows:
        in_specs.append(pl.BlockSpec((r.shape[0], tn), lambda j, i: (0, j)))
    in_specs += [pl.BlockSpec(blk, imap) for _, blk, imap in list(extras_p) + list(extras_s)]
    outs = list(outs_p) + list(outs_s)
    kern = functools.partial(
        _ws_matmul_kernel, n_a=a_split, n_w=n_w, n_rows=len(rows), n_ex_p=len(extras_p),
        n_ex_s=len(extras_s),
        n_out_p=len(outs_p), n_out_s=len(outs_s), tn=tn, body_p=body_p, body_s=body_s)
    res = pl.pallas_call(
        kern,
        grid=(nj, m_p // tm),
        in_specs=in_specs,
        out_specs=[pl.BlockSpec(blk, imap) for _, _, blk, imap in outs],
        out_shape=[jax.ShapeDtypeStruct(shape, dt) for shape, dt, _, _ in outs],
        scratch_shapes=[pltpu.VMEM((k, n_w * tn), BF16)] + list(scratch),
        compiler_params=_params(("arbitrary", "arbitrary")),
        name=name,
    )(*([a_p] * a_split), a_s, *([w] * n_w), *rows, *[x for x, _, _ in extras_p],
      *[x for x, _, _ in extras_s])
    return res[:len(outs_p)], res[len(outs_p):]


def _rowwise_body(epilogue):
    def body(zs, m, row_refs, extra_refs, out_refs, scratch, first):
        sub = min(ROW_SUBBLOCK, m)
        row_vals = [r[...] for r in row_refs]
        for r0 in range(0, m, sub):
            for o_ref, r in zip(out_refs, epilogue(zs(slice(r0, r0 + sub)), row_vals)):
                n = r.shape[0]
                lo = r0 * n // sub
                o_ref[lo:lo + n, :] = r.astype(o_ref.dtype)
    return body


def _row_major(m, tm, n_cols, tn, dtype, sample):
    imap = (lambda j, i: (0, j)) if sample else (lambda j, i: (i, j))
    return ((m, n_cols), dtype, (tm, tn), imap)


def _head_major(m, tm, n_heads, tn, dtype, sample, group=1):
    imap = (lambda j, i: (j, 0, 0)) if sample else (lambda j, i: (j, i, 0))
    return ((n_heads, m // group, tn), dtype, (None, tm // group, tn), imap)


def _silu(z):
    return z * jax.nn.sigmoid(z)


def _hgrn_proj_epilogue(zs, row_vals, *, layer, chunk):
    zq, zf, zi, zg = zs
    logits = row_vals[0]
    e = jnp.exp(logits - jnp.max(logits, axis=0, keepdims=True))
    lb = jnp.sum(e[:layer + 1], axis=0, keepdims=True) / jnp.sum(e, axis=0, keepdims=True)
    f = lb + (1.0 - lb) * jax.nn.sigmoid(zf)
    k = (1.0 - lb) * jax.nn.sigmoid(-zf)
    g = jnp.log(f)
    chunk_sums = jnp.sum(g.reshape(g.shape[0] // chunk, chunk, g.shape[1]), axis=1)
    return [_silu(zq), k, g, zi, _silu(zg), chunk_sums]


def _swiglu_epilogue(zs, row_vals):
    return [_silu(zs[0]) * zs[1]]


def _conv_rows(cu, prev2, prev1, w):
    row = lax.broadcasted_iota(jnp.int32, cu.shape, 0)
    c1 = jnp.where(row == 0, prev1, pltpu.roll(cu, 1, 0))
    c2 = jnp.where(row == 0, prev2, jnp.where(row == 1, prev1, pltpu.roll(cu, 2, 0)))
    return w[0:1, :] * c2 + w[1:2, :] * c1 + w[2:3, :] * cu


def _conv_body_prompt(zs, m, row_refs, extra_refs, out_refs, scratch, first):
    (buf_ref,), (gated_ref, state_ref), (carry_ref,) = extra_refs, out_refs, scratch
    w = row_refs[0][...]

    @pl.when(first)
    def _():
        carry_ref[...] = buf_ref[...]

    prev2, prev1 = carry_ref[0:1, :], carry_ref[1:2, :]
    sub = min(ROW_SUBBLOCK, m)
    for r0 in range(0, m, sub):
        gb, gc, u = zs(slice(r0, r0 + sub))
        cu = gc * u
        gated_ref[r0:r0 + sub, :] = (gb * _conv_rows(cu, prev2, prev1, w)).astype(gated_ref.dtype)
        prev2, prev1 = cu[sub - 2:sub - 1, :], cu[sub - 1:sub, :]
    last = jnp.concatenate([prev2, prev1], axis=0)
    carry_ref[...] = last
    state_ref[...] = last


def _conv_body_sample(zs, m, row_refs, extra_refs, out_refs, scratch, first, *, frames):
    (buf_ref,), (gated_ref, state_ref) = extra_refs, out_refs
    w = row_refs[0][...]
    gb, gc, u = zs(slice(0, m))
    cu = gc * u
    for b in range(m // frames):
        rs = slice(b * frames, (b + 1) * frames)
        cub = cu[rs, :]
        buf = buf_ref[b]
        conv = _conv_rows(cub, buf[0:1, :], buf[1:2, :], w)
        gated_ref[rs, :] = (gb[rs, :] * conv).astype(gated_ref.dtype)
        state_ref[b] = cub[frames - 2:frames, :]


def _matmul_kernel(a_ref, w_ref, x_ref, o_ref):
    o_ref[...] = x_ref[...] + jnp.dot(a_ref[...], w_ref[...], preferred_element_type=F32)


def _matmul_residual(a, w, layer, x, tm, tn, k_splits, name):
    m, k = a.shape
    n = w.shape[2]
    tm = min(tm, m)
    tn = min(tn, n)
    tk = k // k_splits
    assert m % tm == 0 and n % tn == 0 and k % k_splits == 0
    for ks in range(k_splits):
        x = pl.pallas_call(
            _matmul_kernel,
            grid=(m // tm, n // tn),
            in_specs=[pl.BlockSpec((tm, tk), lambda i, j, ks=ks: (i, ks)),
                      pl.BlockSpec((None, tk, tn), lambda i, j, ks=ks: (layer, ks, j)),
                      pl.BlockSpec((tm, tn), lambda i, j: (i, j))],
            out_specs=pl.BlockSpec((tm, tn), lambda i, j: (i, j)),
            out_shape=jax.ShapeDtypeStruct((m, n), F32),
            compiler_params=_params(("arbitrary", "arbitrary")),
            name=name,
        )(a, w, x)
    return x


def _neg_abs(x):
    bits = lax.bitcast_convert_type(x, jnp.uint32) | jnp.uint32(0x80000000)
    return lax.bitcast_convert_type(bits, F32)


class _ChunkSums:
    def __init__(self, g, p_ref, chunk):
        self.chunk, self.n, self.p_ref = chunk, chunk // SUBLANES, p_ref
        n = self.n
        self.r8 = lax.broadcasted_iota(jnp.int32, (SUBLANES, HEAD_DIM), 0)
        self.tiles = [g[SUBLANES * j:SUBLANES * (j + 1)] for j in range(n)]
        self.pre = []
        for t in self.tiles:
            p = t
            d = 1
            while d < SUBLANES:
                p = p + jnp.where(self.r8 >= d, pltpu.roll(p, d, 0), 0.0)
                d *= 2
            self.pre.append(p)
        p_ref[...] = jnp.concatenate(self.pre, axis=0)
        tot = [self.row(SUBLANES * j + SUBLANES - 1) for j in range(n)]
        self.suf = [tot[j] - self.pre[j] for j in range(n)]
        self.end = [tot[0]]
        for j in range(1, n):
            self.end.append(self.end[j - 1] + tot[j])

    def row(self, i):
        return jnp.broadcast_to(self.p_ref[i:i + 1, :], (SUBLANES, HEAD_DIM))

    def upper(self, j, jm):
        return self.pre[j] if j - 1 == jm else self.pre[j] + (self.end[j - 1] - self.end[jm])

    def lower(self, j, jm):
        return self.suf[j] if j == jm else self.suf[j] + (self.end[jm] - self.end[j])

    def from_start(self):
        return jnp.concatenate([self.pre[j] + self.end[j - 1] if j else self.pre[0]
                                for j in range(self.n)], axis=0)

    def to_end(self):
        return jnp.concatenate([self.lower(j, self.n - 1) for j in range(self.n)], axis=0)

    def levels(self):
        r8, n = self.r8, self.n
        out = {1: jnp.concatenate([jnp.where((r8 & 1) == 1, t, 0.0) for t in self.tiles], axis=0)}
        m = 2
        while m < SUBLANES:
            level = []
            for j in range(n):
                base = SUBLANES * j
                mid = self.row(base + m - 1)
                for blk in range(1, SUBLANES // (2 * m)):
                    mid = jnp.where(r8 < 2 * m * blk, mid, self.row(base + 2 * m * blk + m - 1))
                level.append(_neg_abs(self.pre[j] - mid))
            out[m] = jnp.concatenate(level, axis=0)
            m *= 2
        while m < self.chunk:
            per = 2 * m // SUBLANES
            level = []
            for j in range(n):
                jm = (j // per) * per + per // 2 - 1
                level.append(self.upper(j, jm) if j > jm else self.lower(j, jm))
            out[m] = jnp.concatenate(level, axis=0)
            m *= 2
        return out


def _hgrn_kernel(direct_ref, q_ref, k_ref, g_ref, v_ref, z_ref, s0_ref, gain_ref, o_ref, sfin_ref,
                 st_ref, p_ref, *, chunk, n_chunks, heads):
    bi, hi, li = pl.program_id(0), pl.program_id(1), pl.program_id(2)
    direct = direct_ref[(bi * pl.num_programs(2) + li) * pl.num_programs(1) + hi] == 1

    @pl.when(li == 0)
    def _():
        for hb in range(heads):
            st_ref[hb] = s0_ref[0, hb].T

    t_idx = lax.broadcasted_iota(jnp.int32, (chunk, chunk), 0)
    s_idx = lax.broadcasted_iota(jnp.int32, (chunk, chunk), 1)
    differ = t_idx ^ s_idx
    masks = []
    m = 1
    while m < chunk:
        masks.append((m, (t_idx > s_idx) & (differ >= m) & (differ < 2 * m)))
        m *= 2
    gain = gain_ref[...]
    nt_dims = (((1,), (1,)), ((), ()))
    tn_dims = (((0,), (0,)), ((), ()))

    def pair_rows(x0, x1):
        zero = jnp.zeros_like(x0)
        return jnp.concatenate([jnp.concatenate([x0, zero], axis=1),
                                jnp.concatenate([zero, x1], axis=1)], axis=0)

    def paired_nt(lhs, rhs):
        out = []
        for i in range(0, len(lhs), 2):
            if i + 1 < len(lhs):
                r = lax.dot_general(pair_rows(lhs[i], lhs[i + 1]),
                                    jnp.concatenate([rhs[i], rhs[i + 1]], axis=1), nt_dims,
                                    preferred_element_type=F32)
                half = lhs[i].shape[0]
                out += [r[:half], r[half:]]
            else:
                out.append(lax.dot_general(lhs[i], rhs[i], nt_dims, preferred_element_type=F32))
        return out

    def finish(rows, a_bf, q_in, k_out, decay):
        sts = [st_ref[hb] for hb in range(heads)]
        o_inter = paired_nt(q_in, [st.astype(BF16) for st in sts])
        for hb in range(heads):
            lanes = slice(hb * HEAD_DIM, (hb + 1) * HEAD_DIM)
            v = v_ref[hb, rows, :]
            o = o_inter[hb] + jnp.dot(a_bf[hb], v, preferred_element_type=F32)
            st_ref[hb] = decay[hb] * sts[hb] + lax.dot_general(v, k_out[hb], tn_dims,
                                                               preferred_element_type=F32)
            o = o * lax.rsqrt(jnp.mean(o * o, axis=-1, keepdims=True) + NORM_EPS) * gain
            o_ref[rows, lanes] = (o * z_ref[hb, rows, :]).astype(o_ref.dtype)

    def chunk_rows(c):
        return pl.ds(pl.multiple_of(c * chunk, chunk), chunk)

    def level_body(c, carry):
        rows = chunk_rows(c)
        a_bf, q_in, k_out, decay = [], [], [], []
        for hb in range(heads):
            q = q_ref[hb, rows, :]
            k = k_ref[hb, rows, :]
            sums = _ChunkSums(g_ref[hb, rows, :], p_ref.at[hb], chunk)
            levels, p_c = sums.levels(), sums.from_start()
            a = jnp.where(t_idx == s_idx, jnp.sum(q * k, axis=-1, keepdims=True), 0.0)
            es = [jnp.exp(levels[m]) for m, _ in masks]
            parts = paired_nt([(q * e).astype(BF16) for e in es], [(k * e).astype(BF16) for e in es])
            for (_, mask), part in zip(masks, parts):
                a = jnp.where(mask, part, a)
            a_bf.append(a.astype(BF16))
            q_in.append((q * jnp.exp(p_c)).astype(BF16))
            k_out.append((k * jnp.exp(sums.to_end())).astype(BF16))
            decay.append(jnp.exp(p_c[chunk - 1:chunk, :]))
        finish(rows, a_bf, q_in, k_out, decay)
        return carry

    def direct_body(c, carry):
        rows = chunk_rows(c)
        q_in, k_neg, k_out, decay = [], [], [], []
        for hb in range(heads):
            q = q_ref[hb, rows, :]
            k = k_ref[hb, rows, :]
            sums = _ChunkSums(g_ref[hb, rows, :], p_ref.at[hb], chunk)
            p_c = sums.from_start()
            q_in.append((q * jnp.exp(p_c)).astype(BF16))
            k_neg.append((k * jnp.exp(-p_c)).astype(BF16))
            k_out.append((k * jnp.exp(sums.to_end())).astype(BF16))
            decay.append(jnp.exp(p_c[chunk - 1:chunk, :]))
        a_bf = [jnp.where(t_idx >= s_idx, a, 0.0).astype(BF16) for a in paired_nt(q_in, k_neg)]
        finish(rows, a_bf, q_in, k_out, decay)
        return carry

    @pl.when(direct)
    def _():
        lax.fori_loop(0, n_chunks, direct_body, 0, unroll=min(4, n_chunks))

    @pl.when(jnp.logical_not(direct))
    def _():
        lax.fori_loop(0, n_chunks, level_body, 0)

    @pl.when(li == pl.num_programs(2) - 1)
    def _():
        for hb in range(heads):
            sfin_ref[0, hb] = st_ref[hb].T


def _hgrn_recurrence(q, k, g, v, z, chunk_sums, s0, gain, n_streams, chunk, block_frames):
    h, m, _ = q.shape
    l = m // n_streams
    t = min(block_frames, l)
    hb = min(HEADS_PER_STEP, h)
    assert l % t == 0 and t % chunk == 0 and h % hb == 0
    nl = l // t
    least = jnp.min(chunk_sums.reshape(h // hb, hb, n_streams * nl, t // chunk, HEAD_DIM),
                    axis=(1, 3, 4))
    direct = (least.T >= DIRECT_DECAY_MIN_LOG).astype(jnp.int32).reshape(-1)
    seq = pl.BlockSpec((hb, t, HEAD_DIM), lambda bi, hi, li, flags: (hi, bi * nl + li, 0))
    state = pl.BlockSpec((1, hb, HEAD_DIM, HEAD_DIM), lambda bi, hi, li, flags: (bi, hi, 0, 0))
    kern = functools.partial(_hgrn_kernel, chunk=chunk, n_chunks=t // chunk, heads=hb)
    return pl.pallas_call(
        kern,
        grid_spec=pltpu.PrefetchScalarGridSpec(
            num_scalar_prefetch=1,
            grid=(n_streams, h // hb, nl),
            in_specs=[seq, seq, seq, seq, seq, state,
                      pl.BlockSpec((1, HEAD_DIM), lambda bi, hi, li, flags: (0, 0))],
            out_specs=[pl.BlockSpec((t, hb * HEAD_DIM),
                                    lambda bi, hi, li, flags: (bi * nl + li, hi)), state],
            scratch_shapes=[pltpu.VMEM((hb, HEAD_DIM, HEAD_DIM), F32),
                            pltpu.VMEM((hb, chunk, HEAD_DIM), F32)]),
        out_shape=[jax.ShapeDtypeStruct((m, h * HEAD_DIM), BF16),
                   jax.ShapeDtypeStruct(s0.shape, s0.dtype)],
        compiler_params=_params(("arbitrary", "arbitrary", "arbitrary")),
        name="hgrn_recurrence",
    )(direct, q, k, g, v, z, s0, gain.reshape(1, HEAD_DIM).astype(F32))


def _ffn(x_p, x_s, norm_gain, w_in, w_out_bf16, layer, d_ff):
    h_p = _rmsnorm(x_p, norm_gain, BF16)
    h_s = _rmsnorm(x_s, norm_gain, BF16)
    m_p, m_s = x_p.shape[0], x_s.shape[0]
    tm, tn = min(1024, m_p), 256
    body = _rowwise_body(_swiglu_epilogue)
    (act_p,), (act_s,) = _ws_matmul(
        h_p, h_s, w_in, layer, (0, d_ff), tn, [], [], [],
        [_row_major(m_p, tm, d_ff, tn, BF16, False)], [_row_major(m_s, m_s, d_ff, tn, BF16, True)],
        body, body, tm, "ffn_in", a_split=2 if layer == 0 else 4)
    x_p = _matmul_residual(act_p, w_out_bf16, layer, x_p, 1024, 512, 2, "ffn_out")
    x_s = _matmul_residual(act_s, w_out_bf16, layer, x_s, 1024, 512, 2, "ffn_out")
    return x_p, x_s


def _mixer_out(a_p, a_s, w, x_p, x_s, name):
    w_bf16 = w.astype(BF16)
    x_p = _matmul_residual(a_p, w_bf16, 0, x_p, 1024, 512, 1, name)
    x_s = _matmul_residual(a_s, w_bf16, 0, x_s, 1024, 512, 1, name)
    return x_p, x_s


def kernel(x_prompt, x_sample, state_hgrn, state_conv, hgrn_lb_logits, hgrn_w_in, hgrn_out_gain,
           hgrn_w_out, conv_w_in, conv_w, conv_w_out, norm_mix, norm_ffn, ffn_w_in, ffn_w_out,
           norm_final):
    bp, lp, d = x_prompt.shape
    bs, ls, _ = x_sample.shape
    assert bp == 1
    m_p, m_s = bp * lp, bs * ls
    d_ff = ffn_w_out.shape[1]
    n_heads = d // HEAD_DIM
    x_p = x_prompt.reshape(m_p, d)
    x_s = x_sample.reshape(m_s, d)
    ffn_w_out_bf16 = ffn_w_out.astype(BF16)

    h_p = _rmsnorm(x_p, norm_mix[0], BF16)
    h_s = _rmsnorm(x_s, norm_mix[0], BF16)
    tm = min(1024, m_p)
    proj_dtypes = [F32, F32, F32, BF16, F32]
    head_outs = lambda m, t, chunk, sample: (
        [_head_major(m, t, n_heads, HEAD_DIM, dt, sample) for dt in proj_dtypes]
        + [_head_major(m, t, n_heads, HEAD_DIM, F32, sample, group=chunk)])
    epilogue = lambda chunk: _rowwise_body(
        functools.partial(_hgrn_proj_epilogue, layer=0, chunk=chunk))
    proj_p, proj_s = _ws_matmul(
        h_p, h_s, hgrn_w_in, 0, (0, d, 2 * d, 3 * d), HEAD_DIM, [hgrn_lb_logits.astype(F32)], [], [],
        head_outs(m_p, tm, PROMPT_CHUNK, False), head_outs(m_s, m_s, ls, True),
        epilogue(PROMPT_CHUNK), epilogue(ls), tm, "hgrn_proj", a_split=2)
    zero_state = jnp.zeros((bp,) + state_hgrn.shape[2:], state_hgrn.dtype)
    o_p, hgrn_p = _hgrn_recurrence(*proj_p, zero_state, hgrn_out_gain[0], bp, PROMPT_CHUNK,
                                   block_frames=1024)
    o_s, hgrn_s = _hgrn_recurrence(*proj_s, state_hgrn[0], hgrn_out_gain[0], bs, ls,
                                   block_frames=ls)
    x_p, x_s = _mixer_out(o_p, o_s, hgrn_w_out, x_p, x_s, "hgrn_out")
    x_p, x_s = _ffn(x_p, x_s, norm_ffn[0], ffn_w_in, ffn_w_out_bf16, 0, d_ff)

    h_p = _rmsnorm(x_p, norm_mix[1], BF16)
    h_s = _rmsnorm(x_s, norm_mix[1], BF16)
    tm, tn = min(1024, m_p), 256
    two_rows = lambda b: ((b, 2, tn), lambda j, i: (0, 0, j))
    zero_buf = jnp.zeros((bp,) + state_conv.shape[2:], state_conv.dtype)
    (gated_p, conv_p), (gated_s, conv_s) = _ws_matmul(
        h_p, h_s, conv_w_in, 0, (0, d, 2 * d), tn, [conv_w[0].astype(F32)],
        [(zero_buf, (None, 2, tn), two_rows(None)[1])], [(state_conv[0],) + two_rows(bs)],
        [_row_major(m_p, tm, d, tn, BF16, False),
         ((bp, 2, d), state_conv.dtype, (None, 2, tn), two_rows(None)[1])],
        [_row_major(m_s, m_s, d, tn, BF16, True), ((bs, 2, d), state_conv.dtype) + two_rows(bs)],
        _conv_body_prompt, functools.partial(_conv_body_sample, frames=ls), tm, "conv_in",
        scratch=[pltpu.VMEM((2, tn), F32)], a_split=2)
    x_p, x_s = _mixer_out(gated_p, gated_s, conv_w_out, x_p, x_s, "conv_out")
    x_p, x_s = _ffn(x_p, x_s, norm_ffn[1], ffn_w_in, ffn_w_out_bf16, 1, d_ff)

    y_p = _rmsnorm(x_p, norm_final, F32).reshape(bp, lp, d)
    y_s = _rmsnorm(x_s, norm_final, F32).reshape(bs, ls, d)
    return (y_p, y_s, hgrn_p[None], hgrn_s[None], conv_p[None], conv_s[None])
```

```python
import functools

import jax
import jax.numpy as jnp
from jax import lax
from jax.experimental import pallas as pl
from jax.experimental.pallas import tpu as pltpu

F32 = jnp.float32
BF16 = jnp.bfloat16

NORM_EPS = 1e-6
HEAD_DIM = 128
PROMPT_CHUNK = 64
SUBLANES = 8
HEADS_PER_STEP = 4
ROW_SUBBLOCK = 256
VMEM_LIMIT_BYTES = 56 * 1024 * 1024
DIRECT_DECAY_MIN_LOG = -60.0


def _params(semantics):
    return pltpu.CompilerParams(dimension_semantics=semantics,
                                vmem_limit_bytes=VMEM_LIMIT_BYTES)


def _rmsnorm_kernel(x_ref, g_ref, o_ref):
    x = x_ref[...]
    y = x * lax.rsqrt(jnp.mean(x * x, axis=-1, keepdims=True) + NORM_EPS)
    o_ref[...] = (y * g_ref[...]).astype(o_ref.dtype)


def _rmsnorm(x, gain, out_dtype):
    m, d = x.shape
    tm = min(256, m)
    return pl.pallas_call(
        _rmsnorm_kernel,
        grid=(m // tm,),
        in_specs=[pl.BlockSpec((tm, d), lambda i: (i, 0)),
                  pl.BlockSpec((1, d), lambda i: (0, 0))],
        out_specs=pl.BlockSpec((tm, d), lambda i: (i, 0)),
        out_shape=jax.ShapeDtypeStruct((m, d), out_dtype),
        compiler_params=_params(("arbitrary",)),
        name="rmsnorm",
    )(x, gain.reshape(1, d).astype(F32))


def _ws_matmul_kernel(*refs, n_w, n_rows, n_ex_p, n_ex_s, n_out_p, n_out_s, tn, body_p, body_s,
                      folded_norm):
    it = iter(refs)
    take = lambda n: [next(it) for _ in range(n)]
    (a_p, a_s), w_refs, row_refs = take(2), take(n_w), take(n_rows)
    ssq_p, ssq_s = take(2) if folded_norm else (None, None)
    ex_p, ex_s, outs_p, outs_s = take(n_ex_p), take(n_ex_s), take(n_out_p), take(n_out_s)
    wcat, *scratch = list(it)
    first = pl.program_id(1) == 0

    def products(a_ref, ssq_ref):
        def zs(rs):
            z = jnp.dot(a_ref[rs, :], wcat[...], preferred_element_type=F32)
            if folded_norm:
                mean_sq = jnp.sum(ssq_ref[rs, :], axis=-1, keepdims=True) * (1.0 / a_ref.shape[1])
                z = z * lax.rsqrt(mean_sq + NORM_EPS)
            return [z[:, c * tn:(c + 1) * tn] for c in range(n_w)]
        return zs

    @pl.when(first)
    def _():
        for c, w_ref in enumerate(w_refs):
            wcat[:, c * tn:(c + 1) * tn] = w_ref[...].astype(BF16)
        body_s(products(a_s, ssq_s), a_s.shape[0], row_refs, ex_s, outs_s, scratch, None)

    body_p(products(a_p, ssq_p), a_p.shape[0], row_refs, ex_p, outs_p, scratch, first)


def _ws_matmul(a_p, a_s, w, layer, col_offsets, tn, rows, extras_p, extras_s, outs_p, outs_s,
               body_p, body_s, tm, name, scratch=(), norm=None):
    m_p, k = a_p.shape
    m_s = a_s.shape[0]
    tm = min(tm, m_p)
    n_w = len(col_offsets)
    assert m_p % tm == 0 and all(c % tn == 0 for c in col_offsets)
    nj = rows[0].shape[1] // tn if rows else outs_p[0][0][-1] // tn
    once = pl.Buffered(1)
    in_specs = [pl.BlockSpec((tm, k), lambda j, i: (i, 0)),
                pl.BlockSpec((m_s, k), lambda j, i: (0, 0), pipeline_mode=once)]
    for c in col_offsets:
        in_specs.append(pl.BlockSpec((None, k, tn), lambda j, i, c=c: (layer, 0, c // tn + j)))
    for r in rows:
        in_specs.append(pl.BlockSpec((r.shape[0], tn), lambda j, i: (0, j)))
    norm_args = []
    if norm is not None:
        ssq_p, ssq_s = norm_args = list(norm)
        in_specs += [pl.BlockSpec((tm, ssq_p.shape[1]), lambda j, i: (i, 0)),
                     pl.BlockSpec((m_s, ssq_s.shape[1]), lambda j, i: (0, 0), pipeline_mode=once)]
    in_specs += [pl.BlockSpec(blk, imap) for _, blk, imap in list(extras_p) + list(extras_s)]
    outs = list(outs_p) + list(outs_s)
    kern = functools.partial(
        _ws_matmul_kernel, n_w=n_w, n_rows=len(rows), n_ex_p=len(extras_p), n_ex_s=len(extras_s),
        folded_norm=norm is not None,
        n_out_p=len(outs_p), n_out_s=len(outs_s), tn=tn, body_p=body_p, body_s=body_s)
    res = pl.pallas_call(
        kern,
        grid=(nj, m_p // tm),
        in_specs=in_specs,
        out_specs=[pl.BlockSpec(blk, imap) for _, _, blk, imap in outs],
        out_shape=[jax.ShapeDtypeStruct(shape, dt) for shape, dt, _, _ in outs],
        scratch_shapes=[pltpu.VMEM((k, n_w * tn), BF16)] + list(scratch),
        compiler_params=_params(("arbitrary", "arbitrary")),
        name=name,
    )(a_p, a_s, *([w] * n_w), *rows, *norm_args, *[x for x, _, _ in extras_p],
      *[x for x, _, _ in extras_s])
    return res[:len(outs_p)], res[len(outs_p):]


def _rowwise_body(epilogue):
    def body(zs, m, row_refs, extra_refs, out_refs, scratch, first):
        sub = min(ROW_SUBBLOCK, m)
        row_vals = [r[...] for r in row_refs]
        for r0 in range(0, m, sub):
            for o_ref, r in zip(out_refs, epilogue(zs(slice(r0, r0 + sub)), row_vals)):
                n = r.shape[0]
                lo = r0 * n // sub
                o_ref[lo:lo + n, :] = r.astype(o_ref.dtype)
    return body


def _row_major(m, tm, n_cols, tn, dtype, sample):
    imap = (lambda j, i: (0, j)) if sample else (lambda j, i: (i, j))
    return ((m, n_cols), dtype, (tm, tn), imap)


def _head_major(m, tm, n_heads, tn, dtype, sample, group=1):
    imap = (lambda j, i: (j, 0, 0)) if sample else (lambda j, i: (j, i, 0))
    return ((n_heads, m // group, tn), dtype, (None, tm // group, tn), imap)


def _silu(z):
    return z * jax.nn.sigmoid(z)


def _hgrn_proj_epilogue(zs, row_vals, *, layer, chunk):
    zq, zf, zi, zg = zs
    logits = row_vals[0]
    e = jnp.exp(logits - jnp.max(logits, axis=0, keepdims=True))
    lb = jnp.sum(e[:layer + 1], axis=0, keepdims=True) / jnp.sum(e, axis=0, keepdims=True)
    f = lb + (1.0 - lb) * jax.nn.sigmoid(zf)
    k = (1.0 - lb) * jax.nn.sigmoid(-zf)
    g = jnp.log(f)
    chunk_sums = jnp.sum(g.reshape(g.shape[0] // chunk, chunk, g.shape[1]), axis=1)
    return [_silu(zq), k, g, zi, _silu(zg), chunk_sums]


def _swiglu_epilogue(zs, row_vals):
    return [_silu(zs[0]) * zs[1]]


def _conv_rows(cu, prev2, prev1, w):
    row = lax.broadcasted_iota(jnp.int32, cu.shape, 0)
    c1 = jnp.where(row == 0, prev1, pltpu.roll(cu, 1, 0))
    c2 = jnp.where(row == 0, prev2, jnp.where(row == 1, prev1, pltpu.roll(cu, 2, 0)))
    return w[0:1, :] * c2 + w[1:2, :] * c1 + w[2:3, :] * cu


def _conv_body_prompt(zs, m, row_refs, extra_refs, out_refs, scratch, first):
    (buf_ref,), (gated_ref, state_ref), (carry_ref,) = extra_refs, out_refs, scratch
    w = row_refs[0][...]

    @pl.when(first)
    def _():
        carry_ref[...] = buf_ref[...]

    prev2, prev1 = carry_ref[0:1, :], carry_ref[1:2, :]
    sub = min(ROW_SUBBLOCK, m)
    for r0 in range(0, m, sub):
        gb, gc, u = zs(slice(r0, r0 + sub))
        cu = gc * u
        gated_ref[r0:r0 + sub, :] = (gb * _conv_rows(cu, prev2, prev1, w)).astype(gated_ref.dtype)
        prev2, prev1 = cu[sub - 2:sub - 1, :], cu[sub - 1:sub, :]
    last = jnp.concatenate([prev2, prev1], axis=0)
    carry_ref[...] = last
    state_ref[...] = last


def _conv_body_sample(zs, m, row_refs, extra_refs, out_refs, scratch, first, *, frames):
    (buf_ref,), (gated_ref, state_ref) = extra_refs, out_refs
    w = row_refs[0][...]
    gb, gc, u = zs(slice(0, m))
    cu = gc * u
    for b in range(m // frames):
        rs = slice(b * frames, (b + 1) * frames)
        cub = cu[rs, :]
        buf = buf_ref[b]
        conv = _conv_rows(cub, buf[0:1, :], buf[1:2, :], w)
        gated_ref[rs, :] = (gb[rs, :] * conv).astype(gated_ref.dtype)
        state_ref[b] = cub[frames - 2:frames, :]


def _matmul_kernel(a_ref, w_ref, x_ref, *refs):
    m = a_ref.shape[0]
    sub = min(ROW_SUBBLOCK, m)
    parts = []
    for r0 in range(0, m, sub):
        rs = slice(r0, r0 + sub)
        x = x_ref[rs, :] + jnp.dot(a_ref[rs, :], w_ref[...], preferred_element_type=F32)
        if len(refs) == 1:
            (o_ref,) = refs
        else:
            gain_ref, o_ref, xb_ref, ssq_ref = refs
            xb_ref[rs, :] = (x * gain_ref[...]).astype(BF16)
            sq = x * x
            lanes = ssq_ref.shape[1]
            part = sq[:, 0:lanes]
            for c in range(lanes, sq.shape[1], lanes):
                part = part + sq[:, c:c + lanes]
            parts.append(part)
        o_ref[rs, :] = x
    if parts:
        total = jnp.concatenate(parts, axis=0)

        @pl.when(pl.program_id(1) == 0)
        def _():
            ssq_ref[...] = total

        @pl.when(pl.program_id(1) > 0)
        def _():
            ssq_ref[...] += total


def _matmul_residual(a, w, layer, x, tm, tn, k_splits, name, norm_gain=None):
    m, k = a.shape
    n = w.shape[2]
    tm = min(tm, m)
    tn = min(tn, n)
    tk = k // k_splits
    assert m % tm == 0 and n % tn == 0 and k % k_splits == 0
    lanes = min(HEAD_DIM, tn)
    for ks in range(k_splits):
        tile = pl.BlockSpec((tm, tn), lambda i, j: (i, j))
        out_specs, out_shape = [tile], [jax.ShapeDtypeStruct((m, n), F32)]
        in_specs = [pl.BlockSpec((tm, tk), lambda i, j, ks=ks: (i, ks)),
                    pl.BlockSpec((None, tk, tn), lambda i, j, ks=ks: (layer, ks, j)), tile]
        args = [a, w, x]
        if norm_gain is not None and ks == k_splits - 1:
            in_specs.append(pl.BlockSpec((1, tn), lambda i, j: (0, j)))
            args.append(norm_gain.reshape(1, n).astype(F32))
            out_specs += [tile, pl.BlockSpec((tm, lanes), lambda i, j: (i, 0))]
            out_shape += [jax.ShapeDtypeStruct((m, n), BF16), jax.ShapeDtypeStruct((m, lanes), F32)]
        res = pl.pallas_call(
            _matmul_kernel,
            grid=(m // tm, n // tn),
            in_specs=in_specs,
            out_specs=out_specs,
            out_shape=out_shape,
            compiler_params=_params(("arbitrary", "arbitrary")),
            name=name,
        )(*args)
        x = res[0]
    return x if norm_gain is None else tuple(res)


def _neg_abs(x):
    bits = lax.bitcast_convert_type(x, jnp.uint32) | jnp.uint32(0x80000000)
    return lax.bitcast_convert_type(bits, F32)


class _ChunkSums:
    def __init__(self, g, p_ref, chunk):
        self.chunk, self.n, self.p_ref = chunk, chunk // SUBLANES, p_ref
        n = self.n
        self.r8 = lax.broadcasted_iota(jnp.int32, (SUBLANES, HEAD_DIM), 0)
        self.tiles = [g[SUBLANES * j:SUBLANES * (j + 1)] for j in range(n)]
        self.pre = []
        for t in self.tiles:
            p = t
            d = 1
            while d < SUBLANES:
                p = p + jnp.where(self.r8 >= d, pltpu.roll(p, d, 0), 0.0)
                d *= 2
            self.pre.append(p)
        p_ref[...] = jnp.concatenate(self.pre, axis=0)
        tot = [self.row(SUBLANES * j + SUBLANES - 1) for j in range(n)]
        self.suf = [tot[j] - self.pre[j] for j in range(n)]
        self.end = [tot[0]]
        for j in range(1, n):
            self.end.append(self.end[j - 1] + tot[j])

    def row(self, i):
        return jnp.broadcast_to(self.p_ref[i:i + 1, :], (SUBLANES, HEAD_DIM))

    def upper(self, j, jm):
        return self.pre[j] if j - 1 == jm else self.pre[j] + (self.end[j - 1] - self.end[jm])

    def lower(self, j, jm):
        return self.suf[j] if j == jm else self.suf[j] + (self.end[jm] - self.end[j])

    def from_start(self):
        return jnp.concatenate([self.pre[j] + self.end[j - 1] if j else self.pre[0]
                                for j in range(self.n)], axis=0)

    def to_end(self):
        return jnp.concatenate([self.lower(j, self.n - 1) for j in range(self.n)], axis=0)

    def levels(self):
        r8, n = self.r8, self.n
        out = {1: jnp.concatenate([jnp.where((r8 & 1) == 1, t, 0.0) for t in self.tiles], axis=0)}
        m = 2
        while m < SUBLANES:
            level = []
            for j in range(n):
                base = SUBLANES * j
                mid = self.row(base + m - 1)
                for blk in range(1, SUBLANES // (2 * m)):
                    mid = jnp.where(r8 < 2 * m * blk, mid, self.row(base + 2 * m * blk + m - 1))
                level.append(_neg_abs(self.pre[j] - mid))
            out[m] = jnp.concatenate(level, axis=0)
            m *= 2
        while m < self.chunk:
            per = 2 * m // SUBLANES
            level = []
            for j in range(n):
                jm = (j // per) * per + per // 2 - 1
                level.append(self.upper(j, jm) if j > jm else self.lower(j, jm))
            out[m] = jnp.concatenate(level, axis=0)
            m *= 2
        return out


def _hgrn_kernel(direct_ref, q_ref, k_ref, g_ref, v_ref, z_ref, s0_ref, gain_ref, o_ref, sfin_ref,
                 st_ref, p_ref, *, chunk, n_chunks, heads):
    bi, hi, li = pl.program_id(0), pl.program_id(1), pl.program_id(2)
    direct = direct_ref[(bi * pl.num_programs(2) + li) * pl.num_programs(1) + hi] == 1

    @pl.when(li == 0)
    def _():
        for hb in range(heads):
            st_ref[hb] = s0_ref[0, hb].T

    t_idx = lax.broadcasted_iota(jnp.int32, (chunk, chunk), 0)
    s_idx = lax.broadcasted_iota(jnp.int32, (chunk, chunk), 1)
    differ = t_idx ^ s_idx
    masks = []
    m = 1
    while m < chunk:
        masks.append((m, (t_idx > s_idx) & (differ >= m) & (differ < 2 * m)))
        m *= 2
    gain = gain_ref[...]
    nt_dims = (((1,), (1,)), ((), ()))
    tn_dims = (((0,), (0,)), ((), ()))

    def pair_rows(x0, x1):
        zero = jnp.zeros_like(x0)
        return jnp.concatenate([jnp.concatenate([x0, zero], axis=1),
                                jnp.concatenate([zero, x1], axis=1)], axis=0)

    def paired_nt(lhs, rhs):
        out = []
        for i in range(0, len(lhs), 2):
            if i + 1 < len(lhs):
                r = lax.dot_general(pair_rows(lhs[i], lhs[i + 1]),
                                    jnp.concatenate([rhs[i], rhs[i + 1]], axis=1), nt_dims,
                                    preferred_element_type=F32)
                half = lhs[i].shape[0]
                out += [r[:half], r[half:]]
            else:
                out.append(lax.dot_general(lhs[i], rhs[i], nt_dims, preferred_element_type=F32))
        return out

    def finish(rows, a_bf, q_in, k_out, decay):
        sts = [st_ref[hb] for hb in range(heads)]
        o_inter = paired_nt(q_in, [st.astype(BF16) for st in sts])
        for hb in range(heads):
            lanes = slice(hb * HEAD_DIM, (hb + 1) * HEAD_DIM)
            v = v_ref[hb, rows, :]
            o = o_inter[hb] + jnp.dot(a_bf[hb], v, preferred_element_type=F32)
            st_ref[hb] = decay[hb] * sts[hb] + lax.dot_general(v, k_out[hb], tn_dims,
                                                               preferred_element_type=F32)
            o = o * lax.rsqrt(jnp.mean(o * o, axis=-1, keepdims=True) + NORM_EPS) * gain
            o_ref[rows, lanes] = (o * z_ref[hb, rows, :]).astype(o_ref.dtype)

    def chunk_rows(c):
        return pl.ds(pl.multiple_of(c * chunk, chunk), chunk)

    def level_body(c, carry):
        rows = chunk_rows(c)
        a_bf, q_in, k_out, decay = [], [], [], []
        for hb in range(heads):
            q = q_ref[hb, rows, :]
            k = k_ref[hb, rows, :]
            sums = _ChunkSums(g_ref[hb, rows, :], p_ref.at[hb], chunk)
            levels, p_c = sums.levels(), sums.from_start()
            a = jnp.where(t_idx == s_idx, jnp.sum(q * k, axis=-1, keepdims=True), 0.0)
            es = [jnp.exp(levels[m]) for m, _ in masks]
            parts = paired_nt([(q * e).astype(BF16) for e in es], [(k * e).astype(BF16) for e in es])
            for (_, mask), part in zip(masks, parts):
                a = jnp.where(mask, part, a)
            a_bf.append(a.astype(BF16))
            q_in.append((q * jnp.exp(p_c)).astype(BF16))
            k_out.append((k * jnp.exp(sums.to_end())).astype(BF16))
            decay.append(jnp.exp(p_c[chunk - 1:chunk, :]))
        finish(rows, a_bf, q_in, k_out, decay)
        return carry

    def direct_body(c, carry):
        rows = chunk_rows(c)
        q_in, k_neg, k_out, decay = [], [], [], []
        for hb in range(heads):
            q = q_ref[hb, rows, :]
            k = k_ref[hb, rows, :]
            sums = _ChunkSums(g_ref[hb, rows, :], p_ref.at[hb], chunk)
            p_c = sums.from_start()
            q_in.append((q * jnp.exp(p_c)).astype(BF16))
            k_neg.append((k * jnp.exp(-p_c)).astype(BF16))
            k_out.append((k * jnp.exp(sums.to_end())).astype(BF16))
            decay.append(jnp.exp(p_c[chunk - 1:chunk, :]))
        a_bf = [jnp.where(t_idx >= s_idx, a, 0.0).astype(BF16) for a in paired_nt(q_in, k_neg)]
        finish(rows, a_bf, q_in, k_out, decay)
        return carry

    @pl.when(direct)
    def _():
        lax.fori_loop(0, n_chunks, direct_body, 0, unroll=min(4, n_chunks))

    @pl.when(jnp.logical_not(direct))
    def _():
        lax.fori_loop(0, n_chunks, level_body, 0)

    @pl.when(li == pl.num_programs(2) - 1)
    def _():
        for hb in range(heads):
            sfin_ref[0, hb] = st_ref[hb].T


def _hgrn_recurrence(q, k, g, v, z, chunk_sums, s0, gain, n_streams, chunk, block_frames):
    h, m, _ = q.shape
    l = m // n_streams
    t = min(block_frames, l)
    hb = min(HEADS_PER_STEP, h)
    assert l % t == 0 and t % chunk == 0 and h % hb == 0
    nl = l // t
    least = jnp.min(chunk_sums.reshape(h // hb, hb, n_streams * nl, t // chunk, HEAD_DIM),
                    axis=(1, 3, 4))
    direct = (least.T >= DIRECT_DECAY_MIN_LOG).astype(jnp.int32).reshape(-1)
    seq = pl.BlockSpec((hb, t, HEAD_DIM), lambda bi, hi, li, flags: (hi, bi * nl + li, 0))
    state = pl.BlockSpec((1, hb, HEAD_DIM, HEAD_DIM), lambda bi, hi, li, flags: (bi, hi, 0, 0))
    kern = functools.partial(_hgrn_kernel, chunk=chunk, n_chunks=t // chunk, heads=hb)
    return pl.pallas_call(
        kern,
        grid_spec=pltpu.PrefetchScalarGridSpec(
            num_scalar_prefetch=1,
            grid=(n_streams, h // hb, nl),
            in_specs=[seq, seq, seq, seq, seq, state,
                      pl.BlockSpec((1, HEAD_DIM), lambda bi, hi, li, flags: (0, 0))],
            out_specs=[pl.BlockSpec((t, hb * HEAD_DIM),
                                    lambda bi, hi, li, flags: (bi * nl + li, hi)), state],
            scratch_shapes=[pltpu.VMEM((hb, HEAD_DIM, HEAD_DIM), F32),
                            pltpu.VMEM((hb, chunk, HEAD_DIM), F32)]),
        out_shape=[jax.ShapeDtypeStruct((m, h * HEAD_DIM), BF16),
                   jax.ShapeDtypeStruct(s0.shape, s0.dtype)],
        compiler_params=_params(("arbitrary", "arbitrary", "arbitrary")),
        name="hgrn_recurrence",
    )(direct, q, k, g, v, z, s0, gain.reshape(1, HEAD_DIM).astype(F32))


def _ffn(xs_p, xs_s, w_in, w_out_bf16, layer, d_ff, next_gain):
    (x_p, xb_p, ssq_p), (x_s, xb_s, ssq_s) = xs_p, xs_s
    m_p, m_s = x_p.shape[0], x_s.shape[0]
    tm, tn = min(1024, m_p), 256
    body = _rowwise_body(_swiglu_epilogue)
    (act_p,), (act_s,) = _ws_matmul(
        xb_p, xb_s, w_in, layer, (0, d_ff), tn, [], [], [],
        [_row_major(m_p, tm, d_ff, tn, BF16, False)], [_row_major(m_s, m_s, d_ff, tn, BF16, True)],
        body, body, tm, "ffn_in", norm=(ssq_p, ssq_s))
    out_p = _matmul_residual(act_p, w_out_bf16, layer, x_p, 1024, 512, 2, "ffn_out", next_gain)
    out_s = _matmul_residual(act_s, w_out_bf16, layer, x_s, 1024, 512, 2, "ffn_out", next_gain)
    return out_p, out_s


def _mixer_out(a_p, a_s, w, x_p, x_s, next_gain, name):
    w_bf16 = w.astype(BF16)
    out_p = _matmul_residual(a_p, w_bf16, 0, x_p, 1024, 512, 1, name, next_gain)
    out_s = _matmul_residual(a_s, w_bf16, 0, x_s, 1024, 512, 1, name, next_gain)
    return out_p, out_s


def kernel(x_prompt, x_sample, state_hgrn, state_conv, hgrn_lb_logits, hgrn_w_in, hgrn_out_gain,
           hgrn_w_out, conv_w_in, conv_w, conv_w_out, norm_mix, norm_ffn, ffn_w_in, ffn_w_out,
           norm_final):
    bp, lp, d = x_prompt.shape
    bs, ls, _ = x_sample.shape
    assert bp == 1
    m_p, m_s = bp * lp, bs * ls
    d_ff = ffn_w_out.shape[1]
    n_heads = d // HEAD_DIM
    x_p = x_prompt.reshape(m_p, d)
    x_s = x_sample.reshape(m_s, d)
    ffn_w_out_bf16 = ffn_w_out.astype(BF16)

    h_p = _rmsnorm(x_p, norm_mix[0], BF16)
    h_s = _rmsnorm(x_s, norm_mix[0], BF16)
    tm = min(1024, m_p)
    proj_dtypes = [F32, F32, F32, BF16, F32]
    head_outs = lambda m, t, chunk, sample: (
        [_head_major(m, t, n_heads, HEAD_DIM, dt, sample) for dt in proj_dtypes]
        + [_head_major(m, t, n_heads, HEAD_DIM, F32, sample, group=chunk)])
    epilogue = lambda chunk: _rowwise_body(
        functools.partial(_hgrn_proj_epilogue, layer=0, chunk=chunk))
    proj_p, proj_s = _ws_matmul(
        h_p, h_s, hgrn_w_in, 0, (0, d, 2 * d, 3 * d), HEAD_DIM, [hgrn_lb_logits.astype(F32)], [], [],
        head_outs(m_p, tm, PROMPT_CHUNK, False), head_outs(m_s, m_s, ls, True),
        epilogue(PROMPT_CHUNK), epilogue(ls), tm, "hgrn_proj")
    zero_state = jnp.zeros((bp,) + state_hgrn.shape[2:], state_hgrn.dtype)
    o_p, hgrn_p = _hgrn_recurrence(*proj_p, zero_state, hgrn_out_gain[0], bp, PROMPT_CHUNK,
                                   block_frames=1024)
    o_s, hgrn_s = _hgrn_recurrence(*proj_s, state_hgrn[0], hgrn_out_gain[0], bs, ls,
                                   block_frames=ls)
    xs_p, xs_s = _mixer_out(o_p, o_s, hgrn_w_out, x_p, x_s, norm_ffn[0], "hgrn_out")
    xs_p, xs_s = _ffn(xs_p, xs_s, ffn_w_in, ffn_w_out_bf16, 0, d_ff, next_gain=norm_mix[1])

    (x_p, xb_p, ssq_p), (x_s, xb_s, ssq_s) = xs_p, xs_s
    tm, tn = min(1024, m_p), 256
    two_rows = lambda b: ((b, 2, tn), lambda j, i: (0, 0, j))
    zero_buf = jnp.zeros((bp,) + state_conv.shape[2:], state_conv.dtype)
    (gated_p, conv_p), (gated_s, conv_s) = _ws_matmul(
        xb_p, xb_s, conv_w_in, 0, (0, d, 2 * d), tn, [conv_w[0].astype(F32)],
        [(zero_buf, (None, 2, tn), two_rows(None)[1])], [(state_conv[0],) + two_rows(bs)],
        [_row_major(m_p, tm, d, tn, BF16, False),
         ((bp, 2, d), state_conv.dtype, (None, 2, tn), two_rows(None)[1])],
        [_row_major(m_s, m_s, d, tn, BF16, True), ((bs, 2, d), state_conv.dtype) + two_rows(bs)],
        _conv_body_prompt, functools.partial(_conv_body_sample, frames=ls), tm, "conv_in",
        scratch=[pltpu.VMEM((2, tn), F32)], norm=(ssq_p, ssq_s))
    xs_p, xs_s = _mixer_out(gated_p, gated_s, conv_w_out, x_p, x_s, norm_ffn[1], "conv_out")
    x_p, x_s = _ffn(xs_p, xs_s, ffn_w_in, ffn_w_out_bf16, 1, d_ff, next_gain=None)

    y_p = _rmsnorm(x_p, norm_final, F32).reshape(bp, lp, d)
    y_s = _rmsnorm(x_s, norm_final, F32).reshape(bs, ls, d)
    return (y_p, y_s, hgrn_p[None], hgrn_s[None], conv_p[None], conv_s[None])
```

```python
import functools

import jax
import jax.numpy as jnp
from jax import lax
from jax.experimental import pallas as pl
from jax.experimental.pallas import tpu as pltpu

F32 = jnp.float32
BF16 = jnp.bfloat16

NORM_EPS = 1e-6
HEAD_DIM = 128
PROMPT_CHUNK = 64
SUBLANES = 8
HEADS_PER_STEP = 4
ROW_SUBBLOCK = 256
VMEM_LIMIT_BYTES = 56 * 1024 * 1024
DIRECT_DECAY_MIN_LOG = -60.0


def _params(semantics):
    return pltpu.CompilerParams(dimension_semantics=semantics,
                                vmem_limit_bytes=VMEM_LIMIT_BYTES)


def _rmsnorm_kernel(x_ref, g_ref, o_ref):
    x = x_ref[...]
    y = x * lax.rsqrt(jnp.mean(x * x, axis=-1, keepdims=True) + NORM_EPS)
    o_ref[...] = (y * g_ref[...]).astype(o_ref.dtype)


def _rmsnorm(x, gain, out_dtype):
    m, d = x.shape
    tm = min(256, m)
    return pl.pallas_call(
        _rmsnorm_kernel,
        grid=(m // tm,),
        in_specs=[pl.BlockSpec((tm, d), lambda i: (i, 0)),
                  pl.BlockSpec((1, d), lambda i: (0, 0))],
        out_specs=pl.BlockSpec((tm, d), lambda i: (i, 0)),
        out_shape=jax.ShapeDtypeStruct((m, d), out_dtype),
        compiler_params=_params(("arbitrary",)),
        name="rmsnorm",
    )(x, gain.reshape(1, d).astype(F32))


def _ws_matmul_kernel(*refs, n_w, n_rows, n_ex_p, n_ex_s, n_out_p, n_out_s, tn, body_p, body_s,
                      folded_norm):
    it = iter(refs)
    take = lambda n: [next(it) for _ in range(n)]
    (a_p, a_s), w_refs, row_refs = take(2), take(n_w), take(n_rows)
    ssq_p, ssq_s = take(2) if folded_norm else (None, None)
    ex_p, ex_s, outs_p, outs_s = take(n_ex_p), take(n_ex_s), take(n_out_p), take(n_out_s)
    wcat, *scratch = list(it)
    first = pl.program_id(1) == 0

    def products(a_ref, ssq_ref):
        def zs(rs):
            z = jnp.dot(a_ref[rs, :], wcat[...], preferred_element_type=F32)
            if folded_norm:
                mean_sq = jnp.sum(ssq_ref[rs, :], axis=-1, keepdims=True) * (1.0 / a_ref.shape[1])
                z = z * lax.rsqrt(mean_sq + NORM_EPS)
            return [z[:, c * tn:(c + 1) * tn] for c in range(n_w)]
        return zs

    @pl.when(first)
    def _():
        for c, w_ref in enumerate(w_refs):
            wcat[:, c * tn:(c + 1) * tn] = w_ref[...].astype(BF16)
        body_s(products(a_s, ssq_s), a_s.shape[0], row_refs, ex_s, outs_s, scratch, None)

    body_p(products(a_p, ssq_p), a_p.shape[0], row_refs, ex_p, outs_p, scratch, first)


def _ws_matmul(a_p, a_s, w, layer, col_offsets, tn, rows, extras_p, extras_s, outs_p, outs_s,
               body_p, body_s, tm, name, scratch=(), norm=None):
    m_p, k = a_p.shape
    m_s = a_s.shape[0]
    tm = min(tm, m_p)
    n_w = len(col_offsets)
    assert m_p % tm == 0 and all(c % tn == 0 for c in col_offsets)
    nj = rows[0].shape[1] // tn if rows else outs_p[0][0][-1] // tn
    once = pl.Buffered(1)
    in_specs = [pl.BlockSpec((tm, k), lambda j, i: (i, 0)),
                pl.BlockSpec((m_s, k), lambda j, i: (0, 0), pipeline_mode=once)]
    for c in col_offsets:
        in_specs.append(pl.BlockSpec((None, k, tn), lambda j, i, c=c: (layer, 0, c // tn + j)))
    for r in rows:
        in_specs.append(pl.BlockSpec((r.shape[0], tn), lambda j, i: (0, j)))
    norm_args = []
    if norm is not None:
        ssq_p, ssq_s = norm_args = list(norm)
        in_specs += [pl.BlockSpec((tm, ssq_p.shape[1]), lambda j, i: (i, 0)),
                     pl.BlockSpec((m_s, ssq_s.shape[1]), lambda j, i: (0, 0), pipeline_mode=once)]
    in_specs += [pl.BlockSpec(blk, imap) for _, blk, imap in list(extras_p) + list(extras_s)]
    outs = list(outs_p) + list(outs_s)
    kern = functools.partial(
        _ws_matmul_kernel, n_w=n_w, n_rows=len(rows), n_ex_p=len(extras_p), n_ex_s=len(extras_s),
        folded_norm=norm is not None,
        n_out_p=len(outs_p), n_out_s=len(outs_s), tn=tn, body_p=body_p, body_s=body_s)
    res = pl.pallas_call(
        kern,
        grid=(nj, m_p // tm),
        in_specs=in_specs,
        out_specs=[pl.BlockSpec(blk, imap) for _, _, blk, imap in outs],
        out_shape=[jax.ShapeDtypeStruct(shape, dt) for shape, dt, _, _ in outs],
        scratch_shapes=[pltpu.VMEM((k, n_w * tn), BF16)] + list(scratch),
        compiler_params=_params(("arbitrary", "arbitrary")),
        name=name,
    )(a_p, a_s, *([w] * n_w), *rows, *norm_args, *[x for x, _, _ in extras_p],
      *[x for x, _, _ in extras_s])
    return res[:len(outs_p)], res[len(outs_p):]


def _rowwise_body(epilogue):
    def body(zs, m, row_refs, extra_refs, out_refs, scratch, first):
        sub = min(ROW_SUBBLOCK, m)
        starts = list(range(0, m, sub))
        sizes = [sub] * len(starts)
        if len(starts) > 1 and sub % (2 * HEAD_DIM) == 0:
            starts, sizes = starts + [starts[-1] + sub // 2], sizes[:-1] + [sub // 2] * 2
        row_vals = [r[...] for r in row_refs]
        for r0, size in zip(starts, sizes):
            for o_ref, r in zip(out_refs, epilogue(zs(slice(r0, r0 + size)), row_vals)):
                n = r.shape[0]
                lo = r0 * n // size
                o_ref[lo:lo + n, :] = r.astype(o_ref.dtype)
    return body


def _row_major(m, tm, n_cols, tn, dtype, sample):
    imap = (lambda j, i: (0, j)) if sample else (lambda j, i: (i, j))
    return ((m, n_cols), dtype, (tm, tn), imap)


def _head_major(m, tm, n_heads, tn, dtype, sample, group=1):
    imap = (lambda j, i: (j, 0, 0)) if sample else (lambda j, i: (j, i, 0))
    return ((n_heads, m // group, tn), dtype, (None, tm // group, tn), imap)


def _silu(z):
    return z * jax.nn.sigmoid(z)


def _hgrn_proj_epilogue(zs, row_vals, *, layer, chunk):
    zq, zf, zi, zg = zs
    logits = row_vals[0]
    e = jnp.exp(logits - jnp.max(logits, axis=0, keepdims=True))
    lb = jnp.sum(e[:layer + 1], axis=0, keepdims=True) / jnp.sum(e, axis=0, keepdims=True)
    f = lb + (1.0 - lb) * jax.nn.sigmoid(zf)
    k = 1.0 - f
    g = jnp.log(f)
    chunk_sums = jnp.sum(g.reshape(g.shape[0] // chunk, chunk, g.shape[1]), axis=1)
    return [_silu(zq), k, g, zi, _silu(zg), chunk_sums]


def _swiglu_epilogue(zs, row_vals):
    return [_silu(zs[0]) * zs[1]]


def _conv_rows(cu, prev2, prev1, w):
    row = lax.broadcasted_iota(jnp.int32, cu.shape, 0)
    c1 = jnp.where(row == 0, prev1, pltpu.roll(cu, 1, 0))
    c2 = jnp.where(row == 0, prev2, jnp.where(row == 1, prev1, pltpu.roll(cu, 2, 0)))
    return w[0:1, :] * c2 + w[1:2, :] * c1 + w[2:3, :] * cu


def _conv_body_prompt(zs, m, row_refs, extra_refs, out_refs, scratch, first):
    (buf_ref,), (gated_ref, state_ref), (carry_ref,) = extra_refs, out_refs, scratch
    w = row_refs[0][...]

    @pl.when(first)
    def _():
        carry_ref[...] = buf_ref[...]

    prev2, prev1 = carry_ref[0:1, :], carry_ref[1:2, :]
    sub = min(ROW_SUBBLOCK, m)
    for r0 in range(0, m, sub):
        gb, gc, u = zs(slice(r0, r0 + sub))
        cu = gc * u
        gated_ref[r0:r0 + sub, :] = (gb * _conv_rows(cu, prev2, prev1, w)).astype(gated_ref.dtype)
        prev2, prev1 = cu[sub - 2:sub - 1, :], cu[sub - 1:sub, :]
    last = jnp.concatenate([prev2, prev1], axis=0)
    carry_ref[...] = last
    state_ref[...] = last


def _conv_body_sample(zs, m, row_refs, extra_refs, out_refs, scratch, first, *, frames):
    (buf_ref,), (gated_ref, state_ref) = extra_refs, out_refs
    w = row_refs[0][...]
    gb, gc, u = zs(slice(0, m))
    cu = gc * u
    for b in range(m // frames):
        rs = slice(b * frames, (b + 1) * frames)
        cub = cu[rs, :]
        buf = buf_ref[b]
        conv = _conv_rows(cub, buf[0:1, :], buf[1:2, :], w)
        gated_ref[rs, :] = (gb[rs, :] * conv).astype(gated_ref.dtype)
        state_ref[b] = cub[frames - 2:frames, :]


def _matmul_kernel(a_ref, w_ref, x_ref, *refs):
    m = a_ref.shape[0]
    sub = min(ROW_SUBBLOCK, m)
    parts = []
    for r0 in range(0, m, sub):
        rs = slice(r0, r0 + sub)
        x = x_ref[rs, :] + jnp.dot(a_ref[rs, :], w_ref[...], preferred_element_type=F32)
        if len(refs) == 1:
            (o_ref,) = refs
        else:
            gain_ref, o_ref, xb_ref, ssq_ref = refs
            xb_ref[rs, :] = (x * gain_ref[...]).astype(BF16)
            sq = x * x
            lanes = ssq_ref.shape[1]
            part = sq[:, 0:lanes]
            for c in range(lanes, sq.shape[1], lanes):
                part = part + sq[:, c:c + lanes]
            parts.append(part)
        o_ref[rs, :] = x
    if parts:
        total = jnp.concatenate(parts, axis=0)

        @pl.when(pl.program_id(1) == 0)
        def _():
            ssq_ref[...] = total

        @pl.when(pl.program_id(1) > 0)
        def _():
            ssq_ref[...] += total


def _matmul_residual(a, w, layer, x, tm, tn, k_splits, name, norm_gain=None):
    m, k = a.shape
    n = w.shape[2]
    tm = min(tm, m)
    tn = min(tn, n)
    tk = k // k_splits
    assert m % tm == 0 and n % tn == 0 and k % k_splits == 0
    lanes = min(HEAD_DIM, tn)
    for ks in range(k_splits):
        tile = pl.BlockSpec((tm, tn), lambda i, j: (i, j))
        out_specs, out_shape = [tile], [jax.ShapeDtypeStruct((m, n), F32)]
        in_specs = [pl.BlockSpec((tm, tk), lambda i, j, ks=ks: (i, ks)),
                    pl.BlockSpec((None, tk, tn), lambda i, j, ks=ks: (layer, ks, j)), tile]
        args = [a, w, x]
        if norm_gain is not None and ks == k_splits - 1:
            in_specs.append(pl.BlockSpec((1, tn), lambda i, j: (0, j)))
            args.append(norm_gain.reshape(1, n).astype(F32))
            out_specs += [tile, pl.BlockSpec((tm, lanes), lambda i, j: (i, 0))]
            out_shape += [jax.ShapeDtypeStruct((m, n), BF16), jax.ShapeDtypeStruct((m, lanes), F32)]
        res = pl.pallas_call(
            _matmul_kernel,
            grid=(m // tm, n // tn),
            in_specs=in_specs,
            out_specs=out_specs,
            out_shape=out_shape,
            compiler_params=_params(("arbitrary", "arbitrary")),
            name=name,
        )(*args)
        x = res[0]
    return x if norm_gain is None else tuple(res)


def _neg_abs(x):
    bits = lax.bitcast_convert_type(x, jnp.uint32) | jnp.uint32(0x80000000)
    return lax.bitcast_convert_type(bits, F32)


class _ChunkSums:
    def __init__(self, g, p_ref, chunk):
        self.chunk, self.n, self.p_ref = chunk, chunk // SUBLANES, p_ref
        n = self.n
        self.r8 = lax.broadcasted_iota(jnp.int32, (SUBLANES, HEAD_DIM), 0)
        self.tiles = [g[SUBLANES * j:SUBLANES * (j + 1)] for j in range(n)]
        self.pre = []
        for t in self.tiles:
            p = t
            d = 1
            while d < SUBLANES:
                p = p + jnp.where(self.r8 >= d, pltpu.roll(p, d, 0), 0.0)
                d *= 2
            self.pre.append(p)
        p_ref[...] = jnp.concatenate(self.pre, axis=0)
        tot = [self.row(SUBLANES * j + SUBLANES - 1) for j in range(n)]
        self.suf = [tot[j] - self.pre[j] for j in range(n)]
        self.end = [tot[0]]
        for j in range(1, n):
            self.end.append(self.end[j - 1] + tot[j])

    def row(self, i):
        return jnp.broadcast_to(self.p_ref[i:i + 1, :], (SUBLANES, HEAD_DIM))

    def upper(self, j, jm):
        return self.pre[j] if j - 1 == jm else self.pre[j] + (self.end[j - 1] - self.end[jm])

    def lower(self, j, jm):
        return self.suf[j] if j == jm else self.suf[j] + (self.end[jm] - self.end[j])

    def from_start(self):
        return jnp.concatenate([self.pre[j] + self.end[j - 1] if j else self.pre[0]
                                for j in range(self.n)], axis=0)

    def to_end(self):
        return jnp.concatenate([self.lower(j, self.n - 1) for j in range(self.n)], axis=0)

    def levels(self):
        r8, n = self.r8, self.n
        out = {1: jnp.concatenate([jnp.where((r8 & 1) == 1, t, 0.0) for t in self.tiles], axis=0)}
        m = 2
        while m < SUBLANES:
            level = []
            for j in range(n):
                base = SUBLANES * j
                mid = self.row(base + m - 1)
                for blk in range(1, SUBLANES // (2 * m)):
                    mid = jnp.where(r8 < 2 * m * blk, mid, self.row(base + 2 * m * blk + m - 1))
                level.append(_neg_abs(self.pre[j] - mid))
            out[m] = jnp.concatenate(level, axis=0)
            m *= 2
        while m < self.chunk:
            per = 2 * m // SUBLANES
            level = []
            for j in range(n):
                jm = (j // per) * per + per // 2 - 1
                level.append(self.upper(j, jm) if j > jm else self.lower(j, jm))
            out[m] = jnp.concatenate(level, axis=0)
            m *= 2
        return out


def _hgrn_kernel(direct_ref, q_ref, k_ref, g_ref, v_ref, z_ref, s0_ref, gain_ref, *refs, chunk,
                 n_chunks, heads, n_casts):
    cast_in, (o_ref, sfin_ref), cast_out = refs[:n_casts], refs[n_casts:n_casts + 2], \
        refs[n_casts + 2:2 * n_casts + 2]
    st_ref, p_ref = refs[2 * n_casts + 2:]
    for src_ref, dst_ref in zip(cast_in, cast_out):
        dst_ref[...] = src_ref[...].astype(dst_ref.dtype)
    bi, hi, li = pl.program_id(0), pl.program_id(1), pl.program_id(2)
    direct = direct_ref[(bi * pl.num_programs(2) + li) * pl.num_programs(1) + hi] == 1

    @pl.when(li == 0)
    def _():
        for hb in range(heads):
            st_ref[hb] = s0_ref[0, hb].T

    t_idx = lax.broadcasted_iota(jnp.int32, (chunk, chunk), 0)
    s_idx = lax.broadcasted_iota(jnp.int32, (chunk, chunk), 1)
    differ = t_idx ^ s_idx
    masks = []
    m = 1
    while m < chunk:
        masks.append((m, (t_idx > s_idx) & (differ >= m) & (differ < 2 * m)))
        m *= 2
    gain = gain_ref[...]
    nt_dims = (((1,), (1,)), ((), ()))
    tn_dims = (((0,), (0,)), ((), ()))

    def pair_rows(x0, x1):
        zero = jnp.zeros_like(x0)
        return jnp.concatenate([jnp.concatenate([x0, zero], axis=1),
                                jnp.concatenate([zero, x1], axis=1)], axis=0)

    def paired_nt(lhs, rhs):
        out = []
        for i in range(0, len(lhs), 2):
            if i + 1 < len(lhs):
                r = lax.dot_general(pair_rows(lhs[i], lhs[i + 1]),
                                    jnp.concatenate([rhs[i], rhs[i + 1]], axis=1), nt_dims,
                                    preferred_element_type=F32)
                half = lhs[i].shape[0]
                out += [r[:half], r[half:]]
            else:
                out.append(lax.dot_general(lhs[i], rhs[i], nt_dims, preferred_element_type=F32))
        return out

    def finish(rows, a_bf, q_in, k_out, decay):
        sts = [st_ref[hb] for hb in range(heads)]
        o_inter = paired_nt(q_in, [st.astype(BF16) for st in sts])
        for hb in range(heads):
            lanes = slice(hb * HEAD_DIM, (hb + 1) * HEAD_DIM)
            v = v_ref[hb, rows, :]
            o = o_inter[hb] + jnp.dot(a_bf[hb], v, preferred_element_type=F32)
            st_ref[hb] = decay[hb] * sts[hb] + lax.dot_general(v, k_out[hb], tn_dims,
                                                               preferred_element_type=F32)
            o = o * lax.rsqrt(jnp.mean(o * o, axis=-1, keepdims=True) + NORM_EPS) * gain
            o_ref[rows, lanes] = (o * z_ref[hb, rows, :]).astype(o_ref.dtype)

    def chunk_rows(c):
        return pl.ds(pl.multiple_of(c * chunk, chunk), chunk)

    def level_body(c, carry):
        rows = chunk_rows(c)
        a_bf, q_in, k_out, decay = [], [], [], []
        for hb in range(heads):
            q = q_ref[hb, rows, :]
            k = k_ref[hb, rows, :]
            sums = _ChunkSums(g_ref[hb, rows, :], p_ref.at[hb], chunk)
            levels, p_c = sums.levels(), sums.from_start()
            a = jnp.where(t_idx == s_idx, jnp.sum(q * k, axis=-1, keepdims=True), 0.0)
            es = [jnp.exp(levels[m]) for m, _ in masks]
            parts = paired_nt([(q * e).astype(BF16) for e in es], [(k * e).astype(BF16) for e in es])
            for (_, mask), part in zip(masks, parts):
                a = jnp.where(mask, part, a)
            a_bf.append(a.astype(BF16))
            q_in.append((q * jnp.exp(p_c)).astype(BF16))
            k_out.append((k * jnp.exp(sums.to_end())).astype(BF16))
            decay.append(jnp.exp(p_c[chunk - 1:chunk, :]))
        finish(rows, a_bf, q_in, k_out, decay)
        return carry

    def direct_body(c, carry):
        rows = chunk_rows(c)
        q_in, k_neg, k_out, decay = [], [], [], []
        for hb in range(heads):
            q = q_ref[hb, rows, :]
            k = k_ref[hb, rows, :]
            sums = _ChunkSums(g_ref[hb, rows, :], p_ref.at[hb], chunk)
            p_c = sums.from_start()
            q_in.append((q * jnp.exp(p_c)).astype(BF16))
            k_neg.append((k * jnp.exp(-p_c)).astype(BF16))
            k_out.append((k * jnp.exp(sums.to_end())).astype(BF16))
            decay.append(jnp.exp(p_c[chunk - 1:chunk, :]))
        a_bf = [jnp.where(t_idx >= s_idx, a, 0.0).astype(BF16) for a in paired_nt(q_in, k_neg)]
        finish(rows, a_bf, q_in, k_out, decay)
        return carry

    @pl.when(direct)
    def _():
        lax.fori_loop(0, n_chunks, direct_body, 0, unroll=min(4, n_chunks))

    @pl.when(jnp.logical_not(direct))
    def _():
        lax.fori_loop(0, n_chunks, level_body, 0)

    @pl.when(li == pl.num_programs(2) - 1)
    def _():
        for hb in range(heads):
            sfin_ref[0, hb] = st_ref[hb].T


def _cast_blocks(shape, n_steps):
    n_rows, n_cols = shape
    col_blocks = 1
    while col_blocks <= n_steps:
        row_blocks = n_steps // col_blocks
        if (row_blocks * col_blocks == n_steps and n_rows % row_blocks == 0
                and n_cols % col_blocks == 0 and (n_rows // row_blocks) % (2 * SUBLANES) == 0
                and (n_cols // col_blocks) % HEAD_DIM == 0):
            return n_rows // row_blocks, n_cols // col_blocks
        col_blocks *= 2
    return None


def _hgrn_recurrence(q, k, g, v, z, chunk_sums, s0, gain, n_streams, chunk, block_frames,
                     cast_weights=()):
    h, m, _ = q.shape
    l = m // n_streams
    t = min(block_frames, l)
    hb = min(HEADS_PER_STEP * (2 if l == chunk else 1), h)
    assert l % t == 0 and t % chunk == 0 and h % hb == 0
    nl = l // t
    least = jnp.min(chunk_sums.reshape(h // hb, hb, n_streams * nl, t // chunk, HEAD_DIM),
                    axis=(1, 3, 4))
    direct = (least.T >= DIRECT_DECAY_MIN_LOG).astype(jnp.int32).reshape(-1)
    seq = pl.BlockSpec((hb, t, HEAD_DIM), lambda bi, hi, li, flags: (hi, bi * nl + li, 0))
    state = pl.BlockSpec((1, hb, HEAD_DIM, HEAD_DIM), lambda bi, hi, li, flags: (bi, hi, 0, 0))
    n_groups = h // hb
    n_steps = n_streams * n_groups * nl
    flat = [w.reshape(-1, w.shape[-1]) for w in cast_weights]
    blocks = [_cast_blocks(w.shape, n_steps) for w in flat]
    assert all(b is not None for b in blocks)
    cast_specs = [pl.BlockSpec(b, lambda bi, hi, li, flags, cb=w.shape[1] // b[1]: (
        ((bi * n_groups + hi) * nl + li) // cb, ((bi * n_groups + hi) * nl + li) % cb))
        for w, b in zip(flat, blocks)]
    kern = functools.partial(_hgrn_kernel, chunk=chunk, n_chunks=t // chunk, heads=hb,
                             n_casts=len(flat))
    res = pl.pallas_call(
        kern,
        grid_spec=pltpu.PrefetchScalarGridSpec(
            num_scalar_prefetch=1,
            grid=(n_streams, n_groups, nl),
            in_specs=[seq, seq, seq, seq, seq, state,
                      pl.BlockSpec((1, HEAD_DIM), lambda bi, hi, li, flags: (0, 0))] + cast_specs,
            out_specs=[pl.BlockSpec((t, hb * HEAD_DIM),
                                    lambda bi, hi, li, flags: (bi * nl + li, hi)), state] + cast_specs,
            scratch_shapes=[pltpu.VMEM((hb, HEAD_DIM, HEAD_DIM), F32),
                            pltpu.VMEM((hb, chunk, HEAD_DIM), F32)]),
        out_shape=[jax.ShapeDtypeStruct((m, h * HEAD_DIM), BF16),
                   jax.ShapeDtypeStruct(s0.shape, s0.dtype)]
        + [jax.ShapeDtypeStruct(w.shape, BF16) for w in flat],
        compiler_params=_params(("arbitrary", "arbitrary", "arbitrary")),
        name="hgrn_recurrence",
    )(direct, q, k, g, v, z, s0, gain.reshape(1, HEAD_DIM).astype(F32), *flat)
    return res[0], res[1], [wb.reshape(w.shape) for wb, w in zip(res[2:], cast_weights)]


def _ffn(xs_p, xs_s, w_in, w_out_bf16, layer, d_ff, next_gain):
    (x_p, xb_p, ssq_p), (x_s, xb_s, ssq_s) = xs_p, xs_s
    m_p, m_s = x_p.shape[0], x_s.shape[0]
    tm, tn = min(1024, m_p), 256
    body = _rowwise_body(_swiglu_epilogue)
    (act_p,), (act_s,) = _ws_matmul(
        xb_p, xb_s, w_in, layer, (0, d_ff), tn, [], [], [],
        [_row_major(m_p, tm, d_ff, tn, BF16, False)], [_row_major(m_s, m_s, d_ff, tn, BF16, True)],
        body, body, tm, "ffn_in", norm=(ssq_p, ssq_s))
    out_p = _matmul_residual(act_p, w_out_bf16, layer, x_p, 1024, 512, 2, "ffn_out", next_gain)
    out_s = _matmul_residual(act_s, w_out_bf16, layer, x_s, 1024, 512, 2, "ffn_out", next_gain)
    return out_p, out_s


def _mixer_out(a_p, a_s, w_bf16, x_p, x_s, next_gain, name):
    out_p = _matmul_residual(a_p, w_bf16, 0, x_p, 1024, 512, 1, name, next_gain)
    out_s = _matmul_residual(a_s, w_bf16, 0, x_s, 1024, 512, 1, name, next_gain)
    return out_p, out_s


def kernel(x_prompt, x_sample, state_hgrn, state_conv, hgrn_lb_logits, hgrn_w_in, hgrn_out_gain,
           hgrn_w_out, conv_w_in, conv_w, conv_w_out, norm_mix, norm_ffn, ffn_w_in, ffn_w_out,
           norm_final):
    bp, lp, d = x_prompt.shape
    bs, ls, _ = x_sample.shape
    assert bp == 1
    m_p, m_s = bp * lp, bs * ls
    d_ff = ffn_w_out.shape[1]
    n_heads = d // HEAD_DIM
    x_p = x_prompt.reshape(m_p, d)
    x_s = x_sample.reshape(m_s, d)

    h_p = _rmsnorm(x_p, norm_mix[0], BF16)
    h_s = _rmsnorm(x_s, norm_mix[0], BF16)
    tm = min(1024, m_p)
    proj_dtypes = [F32, F32, F32, BF16, F32]
    head_outs = lambda m, t, chunk, sample: (
        [_head_major(m, t, n_heads, HEAD_DIM, dt, sample) for dt in proj_dtypes]
        + [_head_major(m, t, n_heads, HEAD_DIM, F32, sample, group=chunk)])
    epilogue = lambda chunk: _rowwise_body(
        functools.partial(_hgrn_proj_epilogue, layer=0, chunk=chunk))
    proj_p, proj_s = _ws_matmul(
        h_p, h_s, hgrn_w_in, 0, (0, d, 2 * d, 3 * d), HEAD_DIM, [hgrn_lb_logits.astype(F32)], [], [],
        head_outs(m_p, tm, PROMPT_CHUNK, False), head_outs(m_s, m_s, ls, True),
        epilogue(PROMPT_CHUNK), epilogue(ls), tm, "hgrn_proj")
    zero_state = jnp.zeros((bp,) + state_hgrn.shape[2:], state_hgrn.dtype)
    o_p, hgrn_p, (hgrn_w_out_bf16, conv_w_out_bf16, ffn_w_out_bf16) = _hgrn_recurrence(
        *proj_p, zero_state, hgrn_out_gain[0], bp, PROMPT_CHUNK, block_frames=1024,
        cast_weights=(hgrn_w_out, conv_w_out, ffn_w_out))
    o_s, hgrn_s, _ = _hgrn_recurrence(*proj_s, state_hgrn[0], hgrn_out_gain[0], bs, ls,
                                      block_frames=ls)
    xs_p, xs_s = _mixer_out(o_p, o_s, hgrn_w_out_bf16, x_p, x_s, norm_ffn[0], "hgrn_out")
    xs_p, xs_s = _ffn(xs_p, xs_s, ffn_w_in, ffn_w_out_bf16, 0, d_ff, next_gain=norm_mix[1])

    (x_p, xb_p, ssq_p), (x_s, xb_s, ssq_s) = xs_p, xs_s
    tm, tn = min(1024, m_p), 256
    two_rows = lambda b: ((b, 2, tn), lambda j, i: (0, 0, j))
    zero_buf = jnp.zeros((bp,) + state_conv.shape[2:], state_conv.dtype)
    (gated_p, conv_p), (gated_s, conv_s) = _ws_matmul(
        xb_p, xb_s, conv_w_in, 0, (0, d, 2 * d), tn, [conv_w[0].astype(F32)],
        [(zero_buf, (None, 2, tn), two_rows(None)[1])], [(state_conv[0],) + two_rows(bs)],
        [_row_major(m_p, tm, d, tn, BF16, False),
         ((bp, 2, d), state_conv.dtype, (None, 2, tn), two_rows(None)[1])],
        [_row_major(m_s, m_s, d, tn, BF16, True), ((bs, 2, d), state_conv.dtype) + two_rows(bs)],
        _conv_body_prompt, functools.partial(_conv_body_sample, frames=ls), tm, "conv_in",
        scratch=[pltpu.VMEM((2, tn), F32)], norm=(ssq_p, ssq_s))
    xs_p, xs_s = _mixer_out(gated_p, gated_s, conv_w_out_bf16, x_p, x_s, norm_ffn[1], "conv_out")
    x_p, x_s = _ffn(xs_p, xs_s, ffn_w_in, ffn_w_out_bf16, 1, d_ff, next_gain=None)

    y_p = _rmsnorm(x_p, norm_final, F32).reshape(bp, lp, d)
    y_s = _rmsnorm(x_s, norm_final, F32).reshape(bs, ls, d)
    return (y_p, y_s, hgrn_p[None], hgrn_s[None], conv_p[None], conv_s[None])
```

```python
import functools

import jax
import jax.numpy as jnp
from jax import lax
from jax.experimental import pallas as pl
from jax.experimental.pallas import tpu as pltpu

F32 = jnp.float32
BF16 = jnp.bfloat16

NORM_EPS = 1e-6
HEAD_DIM = 128
PROMPT_CHUNK = 64
SUBLANES = 8
HEADS_PER_STEP = 4
ROW_SUBBLOCK = 256
VMEM_LIMIT_BYTES = 56 * 1024 * 1024
NORM_ROWS = 256
WS_ROWS = 1024
WS_COLS = 256
OUT_ROWS, OUT_COLS = 1024, 512
FFN_OUT_K_SPLITS = 2
RECURRENCE_FRAMES = 1024
DIRECT_DECAY_MIN_LOG = -60.0


def _params(semantics):
    return pltpu.CompilerParams(dimension_semantics=semantics,
                                vmem_limit_bytes=VMEM_LIMIT_BYTES)


def _rmsnorm_kernel(x_ref, g_ref, o_ref):
    x = x_ref[...]
    y = x * lax.rsqrt(jnp.mean(x * x, axis=-1, keepdims=True) + NORM_EPS)
    o_ref[...] = (y * g_ref[...]).astype(o_ref.dtype)


def _rmsnorm(x, gain, out_dtype):
    m, d = x.shape
    tm = min(NORM_ROWS, m)
    return pl.pallas_call(
        _rmsnorm_kernel,
        grid=(m // tm,),
        in_specs=[pl.BlockSpec((tm, d), lambda i: (i, 0)),
                  pl.BlockSpec((1, d), lambda i: (0, 0))],
        out_specs=pl.BlockSpec((tm, d), lambda i: (i, 0)),
        out_shape=jax.ShapeDtypeStruct((m, d), out_dtype),
        compiler_params=_params(("arbitrary",)),
        name="rmsnorm",
    )(x, gain.reshape(1, d).astype(F32))


def _ws_matmul_kernel(*refs, n_w, n_rows, n_ex_p, n_ex_s, n_out_p, n_out_s, tn, body_p, body_s,
                      folded_norm):
    it = iter(refs)
    take = lambda n: [next(it) for _ in range(n)]
    (a_p, a_s), w_refs, row_refs = take(2), take(n_w), take(n_rows)
    ssq_p, ssq_s = take(2) if folded_norm else (None, None)
    ex_p, ex_s, outs_p, outs_s = take(n_ex_p), take(n_ex_s), take(n_out_p), take(n_out_s)
    wcat, *scratch = list(it)
    first = pl.program_id(1) == 0

    def products(a_ref, ssq_ref):
        def zs(rs):
            z = jnp.dot(a_ref[rs, :], wcat[...], preferred_element_type=F32)
            if folded_norm:
                mean_sq = jnp.sum(ssq_ref[rs, :], axis=-1, keepdims=True) * (1.0 / a_ref.shape[1])
                z = z * lax.rsqrt(mean_sq + NORM_EPS)
            return [z[:, c * tn:(c + 1) * tn] for c in range(n_w)]
        return zs

    @pl.when(first)
    def _():
        for c, w_ref in enumerate(w_refs):
            wcat[:, c * tn:(c + 1) * tn] = w_ref[...].astype(BF16)
        body_s(products(a_s, ssq_s), a_s.shape[0], row_refs, ex_s, outs_s, scratch, None)

    body_p(products(a_p, ssq_p), a_p.shape[0], row_refs, ex_p, outs_p, scratch, first)


def _ws_matmul(a_p, a_s, w, layer, col_offsets, tn, rows, extras_p, extras_s, outs_p, outs_s,
               body_p, body_s, tm, name, scratch=(), norm=None):
    m_p, k = a_p.shape
    m_s = a_s.shape[0]
    tm = min(tm, m_p)
    n_w = len(col_offsets)
    assert m_p % tm == 0 and all(c % tn == 0 for c in col_offsets)
    nj = rows[0].shape[1] // tn if rows else outs_p[0][0][-1] // tn
    once = pl.Buffered(1)
    in_specs = [pl.BlockSpec((tm, k), lambda j, i: (i, 0)),
                pl.BlockSpec((m_s, k), lambda j, i: (0, 0), pipeline_mode=once)]
    for c in col_offsets:
        in_specs.append(pl.BlockSpec((None, k, tn), lambda j, i, c=c: (layer, 0, c // tn + j)))
    for r in rows:
        in_specs.append(pl.BlockSpec((r.shape[0], tn), lambda j, i: (0, j)))
    norm_args = []
    if norm is not None:
        ssq_p, ssq_s = norm_args = list(norm)
        in_specs += [pl.BlockSpec((tm, ssq_p.shape[1]), lambda j, i: (i, 0)),
                     pl.BlockSpec((m_s, ssq_s.shape[1]), lambda j, i: (0, 0), pipeline_mode=once)]
    in_specs += [pl.BlockSpec(blk, imap) for _, blk, imap in list(extras_p) + list(extras_s)]
    outs = list(outs_p) + list(outs_s)
    kern = functools.partial(
        _ws_matmul_kernel, n_w=n_w, n_rows=len(rows), n_ex_p=len(extras_p), n_ex_s=len(extras_s),
        folded_norm=norm is not None,
        n_out_p=len(outs_p), n_out_s=len(outs_s), tn=tn, body_p=body_p, body_s=body_s)
    res = pl.pallas_call(
        kern,
        grid=(nj, m_p // tm),
        in_specs=in_specs,
        out_specs=[pl.BlockSpec(blk, imap) for _, _, blk, imap in outs],
        out_shape=[jax.ShapeDtypeStruct(shape, dt) for shape, dt, _, _ in outs],
        scratch_shapes=[pltpu.VMEM((k, n_w * tn), BF16)] + list(scratch),
        compiler_params=_params(("arbitrary", "arbitrary")),
        name=name,
    )(a_p, a_s, *([w] * n_w), *rows, *norm_args, *[x for x, _, _ in extras_p],
      *[x for x, _, _ in extras_s])
    return res[:len(outs_p)], res[len(outs_p):]


def _rowwise_body(epilogue):
    def body(zs, m, row_refs, extra_refs, out_refs, scratch, first):
        sub = min(ROW_SUBBLOCK, m)
        starts = list(range(0, m, sub))
        sizes = [sub] * len(starts)
        if len(starts) > 1 and sub % (2 * HEAD_DIM) == 0:
            starts, sizes = starts + [starts[-1] + sub // 2], sizes[:-1] + [sub // 2] * 2
        row_vals = [r[...] for r in row_refs]
        for r0, size in zip(starts, sizes):
            for o_ref, r in zip(out_refs, epilogue(zs(slice(r0, r0 + size)), row_vals)):
                n = r.shape[0]
                lo = r0 * n // size
                o_ref[lo:lo + n, :] = r.astype(o_ref.dtype)
    return body


def _row_major(m, tm, n_cols, tn, dtype, sample):
    imap = (lambda j, i: (0, j)) if sample else (lambda j, i: (i, j))
    return ((m, n_cols), dtype, (tm, tn), imap)


def _head_major(m, tm, n_heads, tn, dtype, sample, group=1):
    imap = (lambda j, i: (j, 0, 0)) if sample else (lambda j, i: (j, i, 0))
    return ((n_heads, m // group, tn), dtype, (None, tm // group, tn), imap)


def _silu(z):
    return z * jax.nn.sigmoid(z)


def _hgrn_proj_epilogue(zs, row_vals, *, layer, chunk):
    zq, zf, zi, zg = zs
    logits = row_vals[0]
    e = jnp.exp(logits - jnp.max(logits, axis=0, keepdims=True))
    lb = jnp.sum(e[:layer + 1], axis=0, keepdims=True) / jnp.sum(e, axis=0, keepdims=True)
    f = lb + (1.0 - lb) * jax.nn.sigmoid(zf)
    k = 1.0 - f
    g = jnp.log(f)
    chunk_sums = jnp.sum(g.reshape(g.shape[0] // chunk, chunk, g.shape[1]), axis=1)
    return [_silu(zq), k, g, zi, _silu(zg), chunk_sums]


def _swiglu_epilogue(zs, row_vals):
    return [_silu(zs[0]) * zs[1]]


def _conv_rows(cu, prev2, prev1, w):
    row = lax.broadcasted_iota(jnp.int32, cu.shape, 0)
    c1 = jnp.where(row == 0, prev1, pltpu.roll(cu, 1, 0))
    c2 = jnp.where(row == 0, prev2, jnp.where(row == 1, prev1, pltpu.roll(cu, 2, 0)))
    return w[0:1, :] * c2 + w[1:2, :] * c1 + w[2:3, :] * cu


def _conv_body_prompt(zs, m, row_refs, extra_refs, out_refs, scratch, first):
    (buf_ref,), (gated_ref, state_ref), (carry_ref,) = extra_refs, out_refs, scratch
    w = row_refs[0][...]

    @pl.when(first)
    def _():
        carry_ref[...] = buf_ref[...]

    prev2, prev1 = carry_ref[0:1, :], carry_ref[1:2, :]
    sub = min(ROW_SUBBLOCK, m)
    for r0 in range(0, m, sub):
        gb, gc, u = zs(slice(r0, r0 + sub))
        cu = gc * u
        gated_ref[r0:r0 + sub, :] = (gb * _conv_rows(cu, prev2, prev1, w)).astype(gated_ref.dtype)
        prev2, prev1 = cu[sub - 2:sub - 1, :], cu[sub - 1:sub, :]
    last = jnp.concatenate([prev2, prev1], axis=0)
    carry_ref[...] = last
    state_ref[...] = last


def _conv_body_sample(zs, m, row_refs, extra_refs, out_refs, scratch, first, *, frames):
    (buf_ref,), (gated_ref, state_ref) = extra_refs, out_refs
    w = row_refs[0][...]
    gb, gc, u = zs(slice(0, m))
    cu = gc * u
    for b in range(m // frames):
        rs = slice(b * frames, (b + 1) * frames)
        cub = cu[rs, :]
        buf = buf_ref[b]
        conv = _conv_rows(cub, buf[0:1, :], buf[1:2, :], w)
        gated_ref[rs, :] = (gb[rs, :] * conv).astype(gated_ref.dtype)
        state_ref[b] = cub[frames - 2:frames, :]


def _matmul_kernel(a_ref, w_ref, x_ref, *refs):
    m = a_ref.shape[0]
    sub = min(ROW_SUBBLOCK, m)
    parts = []
    for r0 in range(0, m, sub):
        rs = slice(r0, r0 + sub)
        x = x_ref[rs, :] + jnp.dot(a_ref[rs, :], w_ref[...], preferred_element_type=F32)
        if len(refs) == 1:
            (o_ref,) = refs
        else:
            gain_ref, o_ref, xb_ref, ssq_ref = refs
            xb_ref[rs, :] = (x * gain_ref[...]).astype(BF16)
            sq = x * x
            lanes = ssq_ref.shape[1]
            part = sq[:, 0:lanes]
            for c in range(lanes, sq.shape[1], lanes):
                part = part + sq[:, c:c + lanes]
            parts.append(part)
        o_ref[rs, :] = x
    if parts:
        total = jnp.concatenate(parts, axis=0)

        @pl.when(pl.program_id(1) == 0)
        def _():
            ssq_ref[...] = total

        @pl.when(pl.program_id(1) > 0)
        def _():
            ssq_ref[...] += total


def _matmul_residual(a, w, layer, x, tm, tn, k_splits, name, norm_gain=None):
    m, k = a.shape
    n = w.shape[2]
    tm = min(tm, m)
    tn = min(tn, n)
    tk = k // k_splits
    assert m % tm == 0 and n % tn == 0 and k % k_splits == 0
    lanes = min(HEAD_DIM, tn)
    for ks in range(k_splits):
        tile = pl.BlockSpec((tm, tn), lambda i, j: (i, j))
        out_specs, out_shape = [tile], [jax.ShapeDtypeStruct((m, n), F32)]
        in_specs = [pl.BlockSpec((tm, tk), lambda i, j, ks=ks: (i, ks)),
                    pl.BlockSpec((None, tk, tn), lambda i, j, ks=ks: (layer, ks, j)), tile]
        args = [a, w, x]
        if norm_gain is not None and ks == k_splits - 1:
            in_specs.append(pl.BlockSpec((1, tn), lambda i, j: (0, j)))
            args.append(norm_gain.reshape(1, n).astype(F32))
            out_specs += [tile, pl.BlockSpec((tm, lanes), lambda i, j: (i, 0))]
            out_shape += [jax.ShapeDtypeStruct((m, n), BF16), jax.ShapeDtypeStruct((m, lanes), F32)]
        res = pl.pallas_call(
            _matmul_kernel,
            grid=(m // tm, n // tn),
            in_specs=in_specs,
            out_specs=out_specs,
            out_shape=out_shape,
            compiler_params=_params(("arbitrary", "arbitrary")),
            name=name,
        )(*args)
        x = res[0]
    return x if norm_gain is None else tuple(res)


def _neg_abs(x):
    bits = lax.bitcast_convert_type(x, jnp.uint32) | jnp.uint32(0x80000000)
    return lax.bitcast_convert_type(bits, F32)


class _ChunkSums:
    def __init__(self, g, p_ref, chunk):
        self.chunk, self.n, self.p_ref = chunk, chunk // SUBLANES, p_ref
        n = self.n
        self.r8 = lax.broadcasted_iota(jnp.int32, (SUBLANES, HEAD_DIM), 0)
        self.tiles = [g[SUBLANES * j:SUBLANES * (j + 1)] for j in range(n)]
        self.pre = []
        for t in self.tiles:
            p = t
            d = 1
            while d < SUBLANES:
                p = p + jnp.where(self.r8 >= d, pltpu.roll(p, d, 0), 0.0)
                d *= 2
            self.pre.append(p)
        p_ref[...] = jnp.concatenate(self.pre, axis=0)
        tot = [self.row(SUBLANES * j + SUBLANES - 1) for j in range(n)]
        self.suf = [tot[j] - self.pre[j] for j in range(n)]
        self.end = [tot[0]]
        for j in range(1, n):
            self.end.append(self.end[j - 1] + tot[j])

    def row(self, i):
        return jnp.broadcast_to(self.p_ref[i:i + 1, :], (SUBLANES, HEAD_DIM))

    def upper(self, j, jm):
        return self.pre[j] if j - 1 == jm else self.pre[j] + (self.end[j - 1] - self.end[jm])

    def lower(self, j, jm):
        return self.suf[j] if j == jm else self.suf[j] + (self.end[jm] - self.end[j])

    def from_start(self):
        return jnp.concatenate([self.pre[j] + self.end[j - 1] if j else self.pre[0]
                                for j in range(self.n)], axis=0)

    def to_end(self):
        return jnp.concatenate([self.lower(j, self.n - 1) for j in range(self.n)], axis=0)

    def levels(self):
        r8, n = self.r8, self.n
        out = {1: jnp.concatenate([jnp.where((r8 & 1) == 1, t, 0.0) for t in self.tiles], axis=0)}
        m = 2
        while m < SUBLANES:
            level = []
            for j in range(n):
                base = SUBLANES * j
                mid = self.row(base + m - 1)
                for blk in range(1, SUBLANES // (2 * m)):
                    mid = jnp.where(r8 < 2 * m * blk, mid, self.row(base + 2 * m * blk + m - 1))
                level.append(_neg_abs(self.pre[j] - mid))
            out[m] = jnp.concatenate(level, axis=0)
            m *= 2
        while m < self.chunk:
            per = 2 * m // SUBLANES
            level = []
            for j in range(n):
                jm = (j // per) * per + per // 2 - 1
                level.append(self.upper(j, jm) if j > jm else self.lower(j, jm))
            out[m] = jnp.concatenate(level, axis=0)
            m *= 2
        return out


def _hgrn_kernel(direct_ref, q_ref, k_ref, g_ref, v_ref, z_ref, s0_ref, gain_ref, *refs, chunk,
                 n_chunks, heads, n_casts):
    cast_in, (o_ref, sfin_ref), cast_out = refs[:n_casts], refs[n_casts:n_casts + 2], \
        refs[n_casts + 2:2 * n_casts + 2]
    st_ref, p_ref = refs[2 * n_casts + 2:]
    for src_ref, dst_ref in zip(cast_in, cast_out):
        dst_ref[...] = src_ref[...].astype(dst_ref.dtype)
    bi, hi, li = pl.program_id(0), pl.program_id(1), pl.program_id(2)
    direct = direct_ref[(bi * pl.num_programs(2) + li) * pl.num_programs(1) + hi] == 1

    @pl.when(li == 0)
    def _():
        for hb in range(heads):
            st_ref[hb] = s0_ref[0, hb].T

    t_idx = lax.broadcasted_iota(jnp.int32, (chunk, chunk), 0)
    s_idx = lax.broadcasted_iota(jnp.int32, (chunk, chunk), 1)
    differ = t_idx ^ s_idx
    masks = []
    m = 1
    while m < chunk:
        masks.append((m, (t_idx > s_idx) & (differ >= m) & (differ < 2 * m)))
        m *= 2
    gain = gain_ref[...]
    nt_dims = (((1,), (1,)), ((), ()))
    tn_dims = (((0,), (0,)), ((), ()))

    def pair_rows(x0, x1):
        zero = jnp.zeros_like(x0)
        return jnp.concatenate([jnp.concatenate([x0, zero], axis=1),
                                jnp.concatenate([zero, x1], axis=1)], axis=0)

    def paired_nt(lhs, rhs):
        out = []
        for i in range(0, len(lhs), 2):
            if i + 1 < len(lhs):
                r = lax.dot_general(pair_rows(lhs[i], lhs[i + 1]),
                                    jnp.concatenate([rhs[i], rhs[i + 1]], axis=1), nt_dims,
                                    preferred_element_type=F32)
                half = lhs[i].shape[0]
                out += [r[:half], r[half:]]
            else:
                out.append(lax.dot_general(lhs[i], rhs[i], nt_dims, preferred_element_type=F32))
        return out

    def finish(rows, a_bf, q_in, k_out, decay):
        sts = [st_ref[hb] for hb in range(heads)]
        o_inter = paired_nt(q_in, [st.astype(BF16) for st in sts])
        for hb in range(heads):
            lanes = slice(hb * HEAD_DIM, (hb + 1) * HEAD_DIM)
            v = v_ref[hb, rows, :]
            o = o_inter[hb] + jnp.dot(a_bf[hb], v, preferred_element_type=F32)
            st_ref[hb] = decay[hb] * sts[hb] + lax.dot_general(v, k_out[hb], tn_dims,
                                                               preferred_element_type=F32)
            o = o * lax.rsqrt(jnp.mean(o * o, axis=-1, keepdims=True) + NORM_EPS) * gain
            o_ref[rows, lanes] = (o * z_ref[hb, rows, :]).astype(o_ref.dtype)

    def chunk_rows(c):
        return pl.ds(pl.multiple_of(c * chunk, chunk), chunk)

    def level_body(c, carry):
        rows = chunk_rows(c)
        a_bf, q_in, k_out, decay = [], [], [], []
        for hb in range(heads):
            q = q_ref[hb, rows, :]
            k = k_ref[hb, rows, :]
            sums = _ChunkSums(g_ref[hb, rows, :], p_ref.at[hb], chunk)
            levels, p_c = sums.levels(), sums.from_start()
            a = jnp.where(t_idx == s_idx, jnp.sum(q * k, axis=-1, keepdims=True), 0.0)
            es = [jnp.exp(levels[m]) for m, _ in masks]
            parts = paired_nt([(q * e).astype(BF16) for e in es], [(k * e).astype(BF16) for e in es])
            for (_, mask), part in zip(masks, parts):
                a = jnp.where(mask, part, a)
            a_bf.append(a.astype(BF16))
            q_in.append((q * jnp.exp(p_c)).astype(BF16))
            k_out.append((k * jnp.exp(sums.to_end())).astype(BF16))
            decay.append(jnp.exp(p_c[chunk - 1:chunk, :]))
        finish(rows, a_bf, q_in, k_out, decay)
        return carry

    def direct_body(c, carry):
        rows = chunk_rows(c)
        q_in, k_neg, k_out, decay = [], [], [], []
        for hb in range(heads):
            q = q_ref[hb, rows, :]
            k = k_ref[hb, rows, :]
            sums = _ChunkSums(g_ref[hb, rows, :], p_ref.at[hb], chunk)
            p_c = sums.from_start()
            q_in.append((q * jnp.exp(p_c)).astype(BF16))
            k_neg.append((k * jnp.exp(-p_c)).astype(BF16))
            k_out.append((k * jnp.exp(sums.to_end())).astype(BF16))
            decay.append(jnp.exp(p_c[chunk - 1:chunk, :]))
        a_bf = [jnp.where(t_idx >= s_idx, a, 0.0).astype(BF16) for a in paired_nt(q_in, k_neg)]
        finish(rows, a_bf, q_in, k_out, decay)
        return carry

    @pl.when(direct)
    def _():
        lax.fori_loop(0, n_chunks, direct_body, 0, unroll=min(4, n_chunks))

    @pl.when(jnp.logical_not(direct))
    def _():
        lax.fori_loop(0, n_chunks, level_body, 0)

    @pl.when(li == pl.num_programs(2) - 1)
    def _():
        for hb in range(heads):
            sfin_ref[0, hb] = st_ref[hb].T


def _cast_blocks(shape, n_steps):
    n_rows, n_cols = shape
    col_blocks = 1
    while col_blocks <= n_steps:
        row_blocks = n_steps // col_blocks
        if (row_blocks * col_blocks == n_steps and n_rows % row_blocks == 0
                and n_cols % col_blocks == 0 and (n_rows // row_blocks) % (2 * SUBLANES) == 0
                and (n_cols // col_blocks) % HEAD_DIM == 0):
            return n_rows // row_blocks, n_cols // col_blocks
        col_blocks *= 2
    return None


def _hgrn_recurrence(q, k, g, v, z, chunk_sums, s0, gain, n_streams, chunk, block_frames,
                     cast_weights=()):
    h, m, _ = q.shape
    l = m // n_streams
    t = min(block_frames, l)
    hb = min(HEADS_PER_STEP * (2 if l == chunk else 1), h)
    assert l % t == 0 and t % chunk == 0 and h % hb == 0
    nl = l // t
    least = jnp.min(chunk_sums.reshape(h // hb, hb, n_streams * nl, t // chunk, HEAD_DIM),
                    axis=(1, 3, 4))
    direct = (least.T >= DIRECT_DECAY_MIN_LOG).astype(jnp.int32).reshape(-1)
    seq = pl.BlockSpec((hb, t, HEAD_DIM), lambda bi, hi, li, flags: (hi, bi * nl + li, 0))
    state = pl.BlockSpec((1, hb, HEAD_DIM, HEAD_DIM), lambda bi, hi, li, flags: (bi, hi, 0, 0))
    n_groups = h // hb
    n_steps = n_streams * n_groups * nl
    flat = [w.reshape(-1, w.shape[-1]) for w in cast_weights]
    blocks = [_cast_blocks(w.shape, n_steps) for w in flat]
    assert all(b is not None for b in blocks)
    cast_specs = [pl.BlockSpec(b, lambda bi, hi, li, flags, cb=w.shape[1] // b[1]: (
        ((bi * n_groups + hi) * nl + li) // cb, ((bi * n_groups + hi) * nl + li) % cb))
        for w, b in zip(flat, blocks)]
    kern = functools.partial(_hgrn_kernel, chunk=chunk, n_chunks=t // chunk, heads=hb,
                             n_casts=len(flat))
    res = pl.pallas_call(
        kern,
        grid_spec=pltpu.PrefetchScalarGridSpec(
            num_scalar_prefetch=1,
            grid=(n_streams, n_groups, nl),
            in_specs=[seq, seq, seq, seq, seq, state,
                      pl.BlockSpec((1, HEAD_DIM), lambda bi, hi, li, flags: (0, 0))] + cast_specs,
            out_specs=[pl.BlockSpec((t, hb * HEAD_DIM),
                                    lambda bi, hi, li, flags: (bi * nl + li, hi)), state] + cast_specs,
            scratch_shapes=[pltpu.VMEM((hb, HEAD_DIM, HEAD_DIM), F32),
                            pltpu.VMEM((hb, chunk, HEAD_DIM), F32)]),
        out_shape=[jax.ShapeDtypeStruct((m, h * HEAD_DIM), BF16),
                   jax.ShapeDtypeStruct(s0.shape, s0.dtype)]
        + [jax.ShapeDtypeStruct(w.shape, BF16) for w in flat],
        compiler_params=_params(("arbitrary", "arbitrary", "arbitrary")),
        name="hgrn_recurrence",
    )(direct, q, k, g, v, z, s0, gain.reshape(1, HEAD_DIM).astype(F32), *flat)
    return res[0], res[1], [wb.reshape(w.shape) for wb, w in zip(res[2:], cast_weights)]


def _ffn(xs_p, xs_s, w_in, w_out_bf16, layer, d_ff, next_gain):
    (x_p, xb_p, ssq_p), (x_s, xb_s, ssq_s) = xs_p, xs_s
    m_p, m_s = x_p.shape[0], x_s.shape[0]
    tm, tn = min(WS_ROWS, m_p), WS_COLS
    body = _rowwise_body(_swiglu_epilogue)
    (act_p,), (act_s,) = _ws_matmul(
        xb_p, xb_s, w_in, layer, (0, d_ff), tn, [], [], [],
        [_row_major(m_p, tm, d_ff, tn, BF16, False)], [_row_major(m_s, m_s, d_ff, tn, BF16, True)],
        body, body, tm, "ffn_in", norm=(ssq_p, ssq_s))
    out_p, out_s = (_matmul_residual(act, w_out_bf16, layer, x, OUT_ROWS, OUT_COLS, FFN_OUT_K_SPLITS,
                                     "ffn_out", next_gain) for act, x in ((act_p, x_p), (act_s, x_s)))
    return out_p, out_s


def _mixer_out(a_p, a_s, w_bf16, x_p, x_s, next_gain, name):
    out_p, out_s = (_matmul_residual(a, w_bf16, 0, x, OUT_ROWS, OUT_COLS, 1, name, next_gain)
                    for a, x in ((a_p, x_p), (a_s, x_s)))
    return out_p, out_s


def kernel(x_prompt, x_sample, state_hgrn, state_conv, hgrn_lb_logits, hgrn_w_in, hgrn_out_gain,
           hgrn_w_out, conv_w_in, conv_w, conv_w_out, norm_mix, norm_ffn, ffn_w_in, ffn_w_out,
           norm_final):
    bp, lp, d = x_prompt.shape
    bs, ls, _ = x_sample.shape
    assert bp == 1
    m_p, m_s = bp * lp, bs * ls
    d_ff = ffn_w_out.shape[1]
    n_heads = d // HEAD_DIM
    x_p = x_prompt.reshape(m_p, d)
    x_s = x_sample.reshape(m_s, d)

    h_p = _rmsnorm(x_p, norm_mix[0], BF16)
    h_s = _rmsnorm(x_s, norm_mix[0], BF16)
    tm = min(WS_ROWS, m_p)
    proj_dtypes = [F32, F32, F32, BF16, F32]
    head_outs = lambda m, t, chunk, sample: (
        [_head_major(m, t, n_heads, HEAD_DIM, dt, sample) for dt in proj_dtypes]
        + [_head_major(m, t, n_heads, HEAD_DIM, F32, sample, group=chunk)])
    epilogue = lambda chunk: _rowwise_body(
        functools.partial(_hgrn_proj_epilogue, layer=0, chunk=chunk))
    proj_p, proj_s = _ws_matmul(
        h_p, h_s, hgrn_w_in, 0, (0, d, 2 * d, 3 * d), HEAD_DIM, [hgrn_lb_logits.astype(F32)], [], [],
        head_outs(m_p, tm, PROMPT_CHUNK, False), head_outs(m_s, m_s, ls, True),
        epilogue(PROMPT_CHUNK), epilogue(ls), tm, "hgrn_proj")
    zero_state = jnp.zeros((bp,) + state_hgrn.shape[2:], state_hgrn.dtype)
    o_p, hgrn_p, (hgrn_w_out_bf16, conv_w_out_bf16, ffn_w_out_bf16) = _hgrn_recurrence(
        *proj_p, zero_state, hgrn_out_gain[0], bp, PROMPT_CHUNK, block_frames=RECURRENCE_FRAMES,
        cast_weights=(hgrn_w_out, conv_w_out, ffn_w_out))
    o_s, hgrn_s, _ = _hgrn_recurrence(*proj_s, state_hgrn[0], hgrn_out_gain[0], bs, ls,
                                      block_frames=ls)
    xs_p, xs_s = _mixer_out(o_p, o_s, hgrn_w_out_bf16, x_p, x_s, norm_ffn[0], "hgrn_out")
    xs_p, xs_s = _ffn(xs_p, xs_s, ffn_w_in, ffn_w_out_bf16, 0, d_ff, next_gain=norm_mix[1])

    (x_p, xb_p, ssq_p), (x_s, xb_s, ssq_s) = xs_p, xs_s
    tm, tn = min(WS_ROWS, m_p), WS_COLS
    two_rows = lambda b: ((b, 2, tn), lambda j, i: (0, 0, j))
    zero_buf = jnp.zeros((bp,) + state_conv.shape[2:], state_conv.dtype)
    (gated_p, conv_p), (gated_s, conv_s) = _ws_matmul(
        xb_p, xb_s, conv_w_in, 0, (0, d, 2 * d), tn, [conv_w[0].astype(F32)],
        [(zero_buf, (None, 2, tn), two_rows(None)[1])], [(state_conv[0],) + two_rows(bs)],
        [_row_major(m_p, tm, d, tn, BF16, False),
         ((bp, 2, d), state_conv.dtype, (None, 2, tn), two_rows(None)[1])],
        [_row_major(m_s, m_s, d, tn, BF16, True), ((bs, 2, d), state_conv.dtype) + two_rows(bs)],
        _conv_body_prompt, functools.partial(_conv_body_sample, frames=ls), tm, "conv_in",
        scratch=[pltpu.VMEM((2, tn), F32)], norm=(ssq_p, ssq_s))
    xs_p, xs_s = _mixer_out(gated_p, gated_s, conv_w_out_bf16, x_p, x_s, norm_ffn[1], "conv_out")
    x_p, x_s = _ffn(xs_p, xs_s, ffn_w_in, ffn_w_out_bf16, 1, d_ff, next_gain=None)

    y_p = _rmsnorm(x_p, norm_final, F32).reshape(bp, lp, d)
    y_s = _rmsnorm(x_s, norm_final, F32).reshape(bs, ls, d)
    return (y_p, y_s, hgrn_p[None], hgrn_s[None], conv_p[None], conv_s[None])
```

```python
import functools

import jax
import jax.numpy as jnp
from jax import lax
from jax.experimental import pallas as pl
from jax.experimental.pallas import tpu as pltpu

F32 = jnp.float32
BF16 = jnp.bfloat16

NORM_EPS = 1e-6
HEAD_DIM = 128
PROMPT_CHUNK = 64
SUBLANES = 8
HEADS_PER_STEP = 4
ROW_SUBBLOCK = 256
VMEM_LIMIT_BYTES = 56 * 1024 * 1024
NORM_ROWS = 256
WS_ROWS = 1024
WS_COLS = 256
OUT_ROWS, OUT_COLS = 1024, 512
FFN_OUT_K_SPLITS = 2
RECURRENCE_FRAMES = 1024
DIRECT_DECAY_MIN_LOG = -60.0


def _params(semantics):
    return pltpu.CompilerParams(dimension_semantics=semantics,
                                vmem_limit_bytes=VMEM_LIMIT_BYTES)


def _rmsnorm_kernel(x_ref, g_ref, o_ref):
    x = x_ref[...]
    y = x * lax.rsqrt(jnp.mean(x * x, axis=-1, keepdims=True) + NORM_EPS)
    o_ref[...] = (y * g_ref[...]).astype(o_ref.dtype)


def _rmsnorm(x, gain, out_dtype):
    m, d = x.shape
    tm = min(NORM_ROWS, m)
    return pl.pallas_call(
        _rmsnorm_kernel,
        grid=(m // tm,),
        in_specs=[pl.BlockSpec((tm, d), lambda i: (i, 0)),
                  pl.BlockSpec((1, d), lambda i: (0, 0))],
        out_specs=pl.BlockSpec((tm, d), lambda i: (i, 0)),
        out_shape=jax.ShapeDtypeStruct((m, d), out_dtype),
        compiler_params=_params(("arbitrary",)),
        name="rmsnorm",
    )(x, gain.reshape(1, d).astype(F32))


def _ws_matmul_kernel(*refs, n_w, n_rows, n_ex_p, n_ex_s, n_out_p, n_out_s, tn, body_p, body_s,
                      folded_norm):
    it = iter(refs)
    take = lambda n: [next(it) for _ in range(n)]
    (a_p, a_s), w_refs, row_refs = take(2), take(n_w), take(n_rows)
    ssq_p, ssq_s = take(2) if folded_norm else (None, None)
    ex_p, ex_s, outs_p, outs_s = take(n_ex_p), take(n_ex_s), take(n_out_p), take(n_out_s)
    wcat, *scratch = list(it)
    first = pl.program_id(1) == 0

    def products(a_ref, ssq_ref):
        def zs(rs):
            z = jnp.dot(a_ref[rs, :], wcat[...], preferred_element_type=F32)
            if folded_norm:
                mean_sq = jnp.sum(ssq_ref[rs, :], axis=-1, keepdims=True) * (1.0 / a_ref.shape[1])
                z = z * lax.rsqrt(mean_sq + NORM_EPS)
            return [z[:, c * tn:(c + 1) * tn] for c in range(n_w)]
        return zs

    @pl.when(first)
    def _():
        for c, w_ref in enumerate(w_refs):
            wcat[:, c * tn:(c + 1) * tn] = w_ref[...].astype(BF16)
        body_s(products(a_s, ssq_s), a_s.shape[0], row_refs, ex_s, outs_s, scratch, None)

    body_p(products(a_p, ssq_p), a_p.shape[0], row_refs, ex_p, outs_p, scratch, first)


def _ws_matmul(a_p, a_s, w, layer, col_offsets, tn, rows, extras_p, extras_s, outs_p, outs_s,
               body_p, body_s, tm, name, scratch=(), norm=None):
    m_p, k = a_p.shape
    m_s = a_s.shape[0]
    tm = min(tm, m_p)
    n_w = len(col_offsets)
    assert m_p % tm == 0 and all(c % tn == 0 for c in col_offsets)
    nj = rows[0].shape[1] // tn if rows else outs_p[0][0][-1] // tn
    once = pl.Buffered(1)
    in_specs = [pl.BlockSpec((tm, k), lambda j, i: (i, 0)),
                pl.BlockSpec((m_s, k), lambda j, i: (0, 0), pipeline_mode=once)]
    for c in col_offsets:
        in_specs.append(pl.BlockSpec((None, k, tn), lambda j, i, c=c: (layer, 0, c // tn + j)))
    for r in rows:
        in_specs.append(pl.BlockSpec((r.shape[0], tn), lambda j, i: (0, j)))
    norm_args = []
    if norm is not None:
        ssq_p, ssq_s = norm_args = list(norm)
        in_specs += [pl.BlockSpec((tm, ssq_p.shape[1]), lambda j, i: (i, 0)),
                     pl.BlockSpec((m_s, ssq_s.shape[1]), lambda j, i: (0, 0), pipeline_mode=once)]
    in_specs += [pl.BlockSpec(blk, imap) for _, blk, imap in list(extras_p) + list(extras_s)]
    outs = list(outs_p) + list(outs_s)
    kern = functools.partial(
        _ws_matmul_kernel, n_w=n_w, n_rows=len(rows), n_ex_p=len(extras_p), n_ex_s=len(extras_s),
        folded_norm=norm is not None,
        n_out_p=len(outs_p), n_out_s=len(outs_s), tn=tn, body_p=body_p, body_s=body_s)
    res = pl.pallas_call(
        kern,
        grid=(nj, m_p // tm),
        in_specs=in_specs,
        out_specs=[pl.BlockSpec(blk, imap) for _, _, blk, imap in outs],
        out_shape=[jax.ShapeDtypeStruct(shape, dt) for shape, dt, _, _ in outs],
        scratch_shapes=[pltpu.VMEM((k, n_w * tn), BF16)] + list(scratch),
        compiler_params=_params(("arbitrary", "arbitrary")),
        name=name,
    )(a_p, a_s, *([w] * n_w), *rows, *norm_args, *[x for x, _, _ in extras_p],
      *[x for x, _, _ in extras_s])
    return res[:len(outs_p)], res[len(outs_p):]


def _rowwise_body(epilogue):
    def body(zs, m, row_refs, extra_refs, out_refs, scratch, first):
        sub = min(ROW_SUBBLOCK, m)
        starts = list(range(0, m, sub))
        sizes = [sub] * len(starts)
        if len(starts) > 1 and sub % (2 * HEAD_DIM) == 0:
            starts, sizes = starts + [starts[-1] + sub // 2], sizes[:-1] + [sub // 2] * 2
        row_vals = [r[...] for r in row_refs]
        for r0, size in zip(starts, sizes):
            for o_ref, r in zip(out_refs, epilogue(zs(slice(r0, r0 + size)), row_vals)):
                n = r.shape[0]
                lo = r0 * n // size
                o_ref[lo:lo + n, :] = r.astype(o_ref.dtype)
    return body


def _row_major(m, tm, n_cols, tn, dtype, sample):
    imap = (lambda j, i: (0, j)) if sample else (lambda j, i: (i, j))
    return ((m, n_cols), dtype, (tm, tn), imap)


def _head_major(m, tm, n_heads, tn, dtype, sample, group=1):
    imap = (lambda j, i: (j, 0, 0)) if sample else (lambda j, i: (j, i, 0))
    return ((n_heads, m // group, tn), dtype, (None, tm // group, tn), imap)


def _silu(z):
    return z * jax.nn.sigmoid(z)


def _hgrn_proj_epilogue(zs, row_vals, *, layer, chunk):
    zq, zf, zi, zg = zs
    logits = row_vals[0]
    e = jnp.exp(logits - jnp.max(logits, axis=0, keepdims=True))
    lb = jnp.sum(e[:layer + 1], axis=0, keepdims=True) / jnp.sum(e, axis=0, keepdims=True)
    f = lb + (1.0 - lb) * jax.nn.sigmoid(zf)
    k = 1.0 - f
    g = jnp.log(f)
    chunk_sums = jnp.sum(g.reshape(g.shape[0] // chunk, chunk, g.shape[1]), axis=1)
    return [_silu(zq), k, g, zi, _silu(zg), chunk_sums]


def _swiglu_epilogue(zs, row_vals):
    return [_silu(zs[0]) * zs[1]]


def _conv_rows(cu, prev2, prev1, w):
    row = lax.broadcasted_iota(jnp.int32, cu.shape, 0)
    c1 = jnp.where(row == 0, prev1, pltpu.roll(cu, 1, 0))
    c2 = jnp.where(row == 0, prev2, jnp.where(row == 1, prev1, pltpu.roll(cu, 2, 0)))
    return w[0:1, :] * c2 + w[1:2, :] * c1 + w[2:3, :] * cu


def _conv_body_prompt(zs, m, row_refs, extra_refs, out_refs, scratch, first):
    (buf_ref,), (gated_ref, state_ref), (carry_ref,) = extra_refs, out_refs, scratch
    w = row_refs[0][...]

    @pl.when(first)
    def _():
        carry_ref[...] = buf_ref[...]

    prev2, prev1 = carry_ref[0:1, :], carry_ref[1:2, :]
    sub = min(ROW_SUBBLOCK, m)
    for r0 in range(0, m, sub):
        gb, gc, u = zs(slice(r0, r0 + sub))
        cu = gc * u
        gated_ref[r0:r0 + sub, :] = (gb * _conv_rows(cu, prev2, prev1, w)).astype(gated_ref.dtype)
        prev2, prev1 = cu[sub - 2:sub - 1, :], cu[sub - 1:sub, :]
    last = jnp.concatenate([prev2, prev1], axis=0)
    carry_ref[...] = last
    state_ref[...] = last


def _conv_body_sample(zs, m, row_refs, extra_refs, out_refs, scratch, first, *, frames):
    (buf_ref,), (gated_ref, state_ref) = extra_refs, out_refs
    w = row_refs[0][...]
    gb, gc, u = zs(slice(0, m))
    cu = gc * u
    for b in range(m // frames):
        rs = slice(b * frames, (b + 1) * frames)
        cub = cu[rs, :]
        buf = buf_ref[b]
        conv = _conv_rows(cub, buf[0:1, :], buf[1:2, :], w)
        gated_ref[rs, :] = (gb[rs, :] * conv).astype(gated_ref.dtype)
        state_ref[b] = cub[frames - 2:frames, :]


def _matmul_kernel(a_ref, w_ref, x_ref, *refs):
    m = a_ref.shape[0]
    sub = min(ROW_SUBBLOCK, m)
    parts = []
    for r0 in range(0, m, sub):
        rs = slice(r0, r0 + sub)
        x = x_ref[rs, :] + jnp.dot(a_ref[rs, :], w_ref[...], preferred_element_type=F32)
        if len(refs) == 1:
            (o_ref,) = refs
        else:
            gain_ref, o_ref, xb_ref, ssq_ref = refs
            xb_ref[rs, :] = (x * gain_ref[...]).astype(BF16)
            sq = x * x
            lanes = ssq_ref.shape[1]
            part = sq[:, 0:lanes]
            for c in range(lanes, sq.shape[1], lanes):
                part = part + sq[:, c:c + lanes]
            parts.append(part)
        o_ref[rs, :] = x
    if parts:
        total = jnp.concatenate(parts, axis=0)

        @pl.when(pl.program_id(1) == 0)
        def _():
            ssq_ref[...] = total

        @pl.when(pl.program_id(1) > 0)
        def _():
            ssq_ref[...] += total


def _matmul_residual(a, w, layer, x, tm, tn, k_splits, name, norm_gain=None):
    m, k = a.shape
    n = w.shape[2]
    tm = min(tm, m)
    tn = min(tn, n)
    tk = k // k_splits
    assert m % tm == 0 and n % tn == 0 and k % k_splits == 0
    lanes = min(HEAD_DIM, tn)
    for ks in range(k_splits):
        tile = pl.BlockSpec((tm, tn), lambda i, j: (i, j))
        out_specs, out_shape = [tile], [jax.ShapeDtypeStruct((m, n), F32)]
        in_specs = [pl.BlockSpec((tm, tk), lambda i, j, ks=ks: (i, ks)),
                    pl.BlockSpec((None, tk, tn), lambda i, j, ks=ks: (layer, ks, j)), tile]
        args = [a, w, x]
        if norm_gain is not None and ks == k_splits - 1:
            in_specs.append(pl.BlockSpec((1, tn), lambda i, j: (0, j)))
            args.append(norm_gain.reshape(1, n).astype(F32))
            out_specs += [tile, pl.BlockSpec((tm, lanes), lambda i, j: (i, 0))]
            out_shape += [jax.ShapeDtypeStruct((m, n), BF16), jax.ShapeDtypeStruct((m, lanes), F32)]
        res = pl.pallas_call(
            _matmul_kernel,
            grid=(m // tm, n // tn),
            in_specs=in_specs,
            out_specs=out_specs,
            out_shape=out_shape,
            compiler_params=_params(("arbitrary", "arbitrary")),
            name=name,
        )(*args)
        x = res[0]
    return x if norm_gain is None else tuple(res)


def _neg_abs(x):
    bits = lax.bitcast_convert_type(x, jnp.uint32) | jnp.uint32(0x80000000)
    return lax.bitcast_convert_type(bits, F32)


class _ChunkSums:
    def __init__(self, g, p_ref, chunk):
        self.chunk, self.n, self.p_ref = chunk, chunk // SUBLANES, p_ref
        n = self.n
        self.r8 = lax.broadcasted_iota(jnp.int32, (SUBLANES, HEAD_DIM), 0)
        self.tiles = [g[SUBLANES * j:SUBLANES * (j + 1)] for j in range(n)]
        self.pre = []
        for t in self.tiles:
            p = t
            d = 1
            while d < SUBLANES:
                p = p + jnp.where(self.r8 >= d, pltpu.roll(p, d, 0), 0.0)
                d *= 2
            self.pre.append(p)
        p_ref[...] = jnp.concatenate(self.pre, axis=0)
        tot = [self.row(SUBLANES * j + SUBLANES - 1) for j in range(n)]
        self.suf = [tot[j] - self.pre[j] for j in range(n)]
        self.end = [tot[0]]
        for j in range(1, n):
            self.end.append(self.end[j - 1] + tot[j])

    def row(self, i):
        return jnp.broadcast_to(self.p_ref[i:i + 1, :], (SUBLANES, HEAD_DIM))

    def upper(self, j, jm):
        return self.pre[j] if j - 1 == jm else self.pre[j] + (self.end[j - 1] - self.end[jm])

    def lower(self, j, jm):
        return self.suf[j] if j == jm else self.suf[j] + (self.end[jm] - self.end[j])

    def from_start_and_to_end(self):
        p_c = [self.pre[j] + self.end[j - 1] if j else self.pre[0] for j in range(self.n)]
        x_c = [self.end[-1] - p for p in p_c]
        return jnp.concatenate(p_c, axis=0), jnp.concatenate(x_c, axis=0)

    def levels(self):
        r8, n = self.r8, self.n
        out = {1: jnp.concatenate([jnp.where((r8 & 1) == 1, t, 0.0) for t in self.tiles], axis=0)}
        m = 2
        while m < SUBLANES:
            level = []
            for j in range(n):
                base = SUBLANES * j
                mid = self.row(base + m - 1)
                for blk in range(1, SUBLANES // (2 * m)):
                    mid = jnp.where(r8 < 2 * m * blk, mid, self.row(base + 2 * m * blk + m - 1))
                level.append(_neg_abs(self.pre[j] - mid))
            out[m] = jnp.concatenate(level, axis=0)
            m *= 2
        while m < self.chunk:
            per = 2 * m // SUBLANES
            level = []
            for j in range(n):
                jm = (j // per) * per + per // 2 - 1
                level.append(self.upper(j, jm) if j > jm else self.lower(j, jm))
            out[m] = jnp.concatenate(level, axis=0)
            m *= 2
        return out


def _hgrn_kernel(direct_ref, q_ref, k_ref, g_ref, v_ref, z_ref, s0_ref, gain_ref, *refs, chunk,
                 n_chunks, heads, n_casts):
    cast_in, (o_ref, sfin_ref), cast_out = refs[:n_casts], refs[n_casts:n_casts + 2], \
        refs[n_casts + 2:2 * n_casts + 2]
    st_ref, p_ref = refs[2 * n_casts + 2:]
    for src_ref, dst_ref in zip(cast_in, cast_out):
        dst_ref[...] = src_ref[...].astype(dst_ref.dtype)
    bi, hi, li = pl.program_id(0), pl.program_id(1), pl.program_id(2)
    direct = direct_ref[(bi * pl.num_programs(2) + li) * pl.num_programs(1) + hi] == 1

    @pl.when(li == 0)
    def _():
        for hb in range(heads):
            st_ref[hb] = s0_ref[0, hb].T

    t_idx = lax.broadcasted_iota(jnp.int32, (chunk, chunk), 0)
    s_idx = lax.broadcasted_iota(jnp.int32, (chunk, chunk), 1)
    differ = t_idx ^ s_idx
    masks = []
    m = 1
    while m < chunk:
        masks.append((m, (t_idx > s_idx) & (differ >= m) & (differ < 2 * m)))
        m *= 2
    gain = gain_ref[...]
    nt_dims = (((1,), (1,)), ((), ()))
    tn_dims = (((0,), (0,)), ((), ()))

    def pair_rows(x0, x1):
        zero = jnp.zeros_like(x0)
        return jnp.concatenate([jnp.concatenate([x0, zero], axis=1),
                                jnp.concatenate([zero, x1], axis=1)], axis=0)

    def paired_nt(lhs, rhs):
        out = []
        for i in range(0, len(lhs), 2):
            if i + 1 < len(lhs):
                r = lax.dot_general(pair_rows(lhs[i], lhs[i + 1]),
                                    jnp.concatenate([rhs[i], rhs[i + 1]], axis=1), nt_dims,
                                    preferred_element_type=F32)
                half = lhs[i].shape[0]
                out += [r[:half], r[half:]]
            else:
                out.append(lax.dot_general(lhs[i], rhs[i], nt_dims, preferred_element_type=F32))
        return out

    def finish(rows, a_bf, q_in, k_out, decay):
        sts = [st_ref[hb] for hb in range(heads)]
        o_inter = paired_nt(q_in, [st.astype(BF16) for st in sts])
        for hb in range(heads):
            lanes = slice(hb * HEAD_DIM, (hb + 1) * HEAD_DIM)
            v = v_ref[hb, rows, :]
            o = o_inter[hb] + jnp.dot(a_bf[hb], v, preferred_element_type=F32)
            st_ref[hb] = decay[hb] * sts[hb] + lax.dot_general(v, k_out[hb], tn_dims,
                                                               preferred_element_type=F32)
            o = o * lax.rsqrt(jnp.mean(o * o, axis=-1, keepdims=True) + NORM_EPS) * gain
            o_ref[rows, lanes] = (o * z_ref[hb, rows, :]).astype(o_ref.dtype)

    def chunk_rows(c):
        return pl.ds(pl.multiple_of(c * chunk, chunk), chunk)

    def level_body(c, carry):
        rows = chunk_rows(c)
        a_bf, q_in, k_out, decay = [], [], [], []
        for hb in range(heads):
            q = q_ref[hb, rows, :]
            k = k_ref[hb, rows, :]
            sums = _ChunkSums(g_ref[hb, rows, :], p_ref.at[hb], chunk)
            levels, (p_c, x_c) = sums.levels(), sums.from_start_and_to_end()
            a = jnp.where(t_idx == s_idx, jnp.sum(q * k, axis=-1, keepdims=True), 0.0)
            es = [jnp.exp(levels[m]) for m, _ in masks]
            parts = paired_nt([(q * e).astype(BF16) for e in es], [(k * e).astype(BF16) for e in es])
            for (_, mask), part in zip(masks, parts):
                a = jnp.where(mask, part, a)
            a_bf.append(a.astype(BF16))
            q_in.append((q * jnp.exp(p_c)).astype(BF16))
            k_out.append((k * jnp.exp(x_c)).astype(BF16))
            decay.append(jnp.exp(p_c[chunk - 1:chunk, :]))
        finish(rows, a_bf, q_in, k_out, decay)
        return carry

    def direct_body(c, carry):
        rows = chunk_rows(c)
        q_in, k_neg, k_out, decay = [], [], [], []
        for hb in range(heads):
            q = q_ref[hb, rows, :]
            k = k_ref[hb, rows, :]
            sums = _ChunkSums(g_ref[hb, rows, :], p_ref.at[hb], chunk)
            p_c, x_c = sums.from_start_and_to_end()
            q_in.append((q * jnp.exp(p_c)).astype(BF16))
            k_neg.append((k * jnp.exp(-p_c)).astype(BF16))
            k_out.append((k * jnp.exp(x_c)).astype(BF16))
            decay.append(jnp.exp(p_c[chunk - 1:chunk, :]))
        a_bf = [jnp.where(t_idx >= s_idx, a, 0.0).astype(BF16) for a in paired_nt(q_in, k_neg)]
        finish(rows, a_bf, q_in, k_out, decay)
        return carry

    @pl.when(direct)
    def _():
        lax.fori_loop(0, n_chunks, direct_body, 0, unroll=min(8, n_chunks))

    @pl.when(jnp.logical_not(direct))
    def _():
        lax.fori_loop(0, n_chunks, level_body, 0)

    @pl.when(li == pl.num_programs(2) - 1)
    def _():
        for hb in range(heads):
            sfin_ref[0, hb] = st_ref[hb].T


def _cast_blocks(shape, n_steps):
    n_rows, n_cols = shape
    col_blocks = 1
    while col_blocks <= n_steps:
        row_blocks = n_steps // col_blocks
        if (row_blocks * col_blocks == n_steps and n_rows % row_blocks == 0
                and n_cols % col_blocks == 0 and (n_rows // row_blocks) % (2 * SUBLANES) == 0
                and (n_cols // col_blocks) % HEAD_DIM == 0):
            return n_rows // row_blocks, n_cols // col_blocks
        col_blocks *= 2
    return None


def _hgrn_recurrence(q, k, g, v, z, chunk_sums, s0, gain, n_streams, chunk, block_frames,
                     cast_weights=()):
    h, m, _ = q.shape
    l = m // n_streams
    t = min(block_frames, l)
    hb = min(HEADS_PER_STEP * (2 if l == chunk else 1), h)
    assert l % t == 0 and t % chunk == 0 and h % hb == 0
    nl = l // t
    least = jnp.min(chunk_sums.reshape(h // hb, hb, n_streams * nl, t // chunk, HEAD_DIM),
                    axis=(1, 3, 4))
    direct = (least.T >= DIRECT_DECAY_MIN_LOG).astype(jnp.int32).reshape(-1)
    seq = pl.BlockSpec((hb, t, HEAD_DIM), lambda bi, hi, li, flags: (hi, bi * nl + li, 0))
    state = pl.BlockSpec((1, hb, HEAD_DIM, HEAD_DIM), lambda bi, hi, li, flags: (bi, hi, 0, 0))
    n_groups = h // hb
    n_steps = n_streams * n_groups * nl
    flat = [w.reshape(-1, w.shape[-1]) for w in cast_weights]
    blocks = [_cast_blocks(w.shape, n_steps) for w in flat]
    assert all(b is not None for b in blocks)
    cast_specs = [pl.BlockSpec(b, lambda bi, hi, li, flags, cb=w.shape[1] // b[1]: (
        ((bi * n_groups + hi) * nl + li) // cb, ((bi * n_groups + hi) * nl + li) % cb))
        for w, b in zip(flat, blocks)]
    kern = functools.partial(_hgrn_kernel, chunk=chunk, n_chunks=t // chunk, heads=hb,
                             n_casts=len(flat))
    res = pl.pallas_call(
        kern,
        grid_spec=pltpu.PrefetchScalarGridSpec(
            num_scalar_prefetch=1,
            grid=(n_streams, n_groups, nl),
            in_specs=[seq, seq, seq, seq, seq, state,
                      pl.BlockSpec((1, HEAD_DIM), lambda bi, hi, li, flags: (0, 0))] + cast_specs,
            out_specs=[pl.BlockSpec((t, hb * HEAD_DIM),
                                    lambda bi, hi, li, flags: (bi * nl + li, hi)), state] + cast_specs,
            scratch_shapes=[pltpu.VMEM((hb, HEAD_DIM, HEAD_DIM), F32),
                            pltpu.VMEM((hb, chunk, HEAD_DIM), F32)]),
        out_shape=[jax.ShapeDtypeStruct((m, h * HEAD_DIM), BF16),
                   jax.ShapeDtypeStruct(s0.shape, s0.dtype)]
        + [jax.ShapeDtypeStruct(w.shape, BF16) for w in flat],
        compiler_params=_params(("arbitrary", "arbitrary", "arbitrary")),
        name="hgrn_recurrence",
    )(direct, q, k, g, v, z, s0, gain.reshape(1, HEAD_DIM).astype(F32), *flat)
    return res[0], res[1], [wb.reshape(w.shape) for wb, w in zip(res[2:], cast_weights)]


def _ffn(xs_p, xs_s, w_in, w_out_bf16, layer, d_ff, next_gain):
    (x_p, xb_p, ssq_p), (x_s, xb_s, ssq_s) = xs_p, xs_s
    m_p, m_s = x_p.shape[0], x_s.shape[0]
    tm, tn = min(WS_ROWS, m_p), WS_COLS
    body = _rowwise_body(_swiglu_epilogue)
    (act_p,), (act_s,) = _ws_matmul(
        xb_p, xb_s, w_in, layer, (0, d_ff), tn, [], [], [],
        [_row_major(m_p, tm, d_ff, tn, BF16, False)], [_row_major(m_s, m_s, d_ff, tn, BF16, True)],
        body, body, tm, "ffn_in", norm=(ssq_p, ssq_s))
    out_p, out_s = (_matmul_residual(act, w_out_bf16, layer, x, OUT_ROWS, OUT_COLS, FFN_OUT_K_SPLITS,
                                     "ffn_out", next_gain) for act, x in ((act_p, x_p), (act_s, x_s)))
    return out_p, out_s


def _mixer_out(a_p, a_s, w_bf16, x_p, x_s, next_gain, name):
    out_p, out_s = (_matmul_residual(a, w_bf16, 0, x, OUT_ROWS, OUT_COLS, 1, name, next_gain)
                    for a, x in ((a_p, x_p), (a_s, x_s)))
    return out_p, out_s


def kernel(x_prompt, x_sample, state_hgrn, state_conv, hgrn_lb_logits, hgrn_w_in, hgrn_out_gain,
           hgrn_w_out, conv_w_in, conv_w, conv_w_out, norm_mix, norm_ffn, ffn_w_in, ffn_w_out,
           norm_final):
    bp, lp, d = x_prompt.shape
    bs, ls, _ = x_sample.shape
    assert bp == 1
    m_p, m_s = bp * lp, bs * ls
    d_ff = ffn_w_out.shape[1]
    n_heads = d // HEAD_DIM
    x_p = x_prompt.reshape(m_p, d)
    x_s = x_sample.reshape(m_s, d)

    h_p = _rmsnorm(x_p, norm_mix[0], BF16)
    h_s = _rmsnorm(x_s, norm_mix[0], BF16)
    tm = min(WS_ROWS, m_p)
    proj_dtypes = [F32, F32, F32, BF16, F32]
    head_outs = lambda m, t, chunk, sample: (
        [_head_major(m, t, n_heads, HEAD_DIM, dt, sample) for dt in proj_dtypes]
        + [_head_major(m, t, n_heads, HEAD_DIM, F32, sample, group=chunk)])
    epilogue = lambda chunk: _rowwise_body(
        functools.partial(_hgrn_proj_epilogue, layer=0, chunk=chunk))
    proj_p, proj_s = _ws_matmul(
        h_p, h_s, hgrn_w_in, 0, (0, d, 2 * d, 3 * d), HEAD_DIM, [hgrn_lb_logits.astype(F32)], [], [],
        head_outs(m_p, tm, PROMPT_CHUNK, False), head_outs(m_s, m_s, ls, True),
        epilogue(PROMPT_CHUNK), epilogue(ls), tm, "hgrn_proj")
    zero_state = jnp.zeros((bp,) + state_hgrn.shape[2:], state_hgrn.dtype)
    o_p, hgrn_p, (hgrn_w_out_bf16, conv_w_out_bf16, ffn_w_out_bf16) = _hgrn_recurrence(
        *proj_p, zero_state, hgrn_out_gain[0], bp, PROMPT_CHUNK, block_frames=RECURRENCE_FRAMES,
        cast_weights=(hgrn_w_out, conv_w_out, ffn_w_out))
    o_s, hgrn_s, _ = _hgrn_recurrence(*proj_s, state_hgrn[0], hgrn_out_gain[0], bs, ls,
                                      block_frames=ls)
    xs_p, xs_s = _mixer_out(o_p, o_s, hgrn_w_out_bf16, x_p, x_s, norm_ffn[0], "hgrn_out")
    xs_p, xs_s = _ffn(xs_p, xs_s, ffn_w_in, ffn_w_out_bf16, 0, d_ff, next_gain=norm_mix[1])

    (x_p, xb_p, ssq_p), (x_s, xb_s, ssq_s) = xs_p, xs_s
    tm, tn = min(WS_ROWS, m_p), WS_COLS
    two_rows = lambda b: ((b, 2, tn), lambda j, i: (0, 0, j))
    zero_buf = jnp.zeros((bp,) + state_conv.shape[2:], state_conv.dtype)
    (gated_p, conv_p), (gated_s, conv_s) = _ws_matmul(
        xb_p, xb_s, conv_w_in, 0, (0, d, 2 * d), tn, [conv_w[0].astype(F32)],
        [(zero_buf, (None, 2, tn), two_rows(None)[1])], [(state_conv[0],) + two_rows(bs)],
        [_row_major(m_p, tm, d, tn, BF16, False),
         ((bp, 2, d), state_conv.dtype, (None, 2, tn), two_rows(None)[1])],
        [_row_major(m_s, m_s, d, tn, BF16, True), ((bs, 2, d), state_conv.dtype) + two_rows(bs)],
        _conv_body_prompt, functools.partial(_conv_body_sample, frames=ls), tm, "conv_in",
        scratch=[pltpu.VMEM((2, tn), F32)], norm=(ssq_p, ssq_s))
    xs_p, xs_s = _mixer_out(gated_p, gated_s, conv_w_out_bf16, x_p, x_s, norm_ffn[1], "conv_out")
    x_p, x_s = _ffn(xs_p, xs_s, ffn_w_in, ffn_w_out_bf16, 1, d_ff, next_gain=None)

    y_p = _rmsnorm(x_p, norm_final, F32).reshape(bp, lp, d)
    y_s = _rmsnorm(x_s, norm_final, F32).reshape(bs, ls, d)
    return (y_p, y_s, hgrn_p[None], hgrn_s[None], conv_p[None], conv_s[None])
```

```python
import functools

import jax
import jax.numpy as jnp
from jax import lax
from jax.experimental import pallas as pl
from jax.experimental.pallas import tpu as pltpu

F32 = jnp.float32
BF16 = jnp.bfloat16

NORM_EPS = 1e-6
HEAD_DIM = 128
PROMPT_CHUNK = 64
SUBLANES = 8
HEADS_PER_STEP = 4
ROW_SUBBLOCK = 256
VMEM_LIMIT_BYTES = 56 * 1024 * 1024
NORM_ROWS = 256
WS_ROWS = 1024
WS_COLS = 256
OUT_ROWS, OUT_COLS = 1024, 512
FFN_OUT_K_SPLITS = 2
RECURRENCE_FRAMES = 1024
DIRECT_DECAY_MIN_LOG = -60.0


def _params(semantics):
    return pltpu.CompilerParams(dimension_semantics=semantics,
                                vmem_limit_bytes=VMEM_LIMIT_BYTES)


def _rmsnorm_kernel(x_ref, g_ref, o_ref):
    x = x_ref[...]
    y = x * lax.rsqrt(jnp.mean(x * x, axis=-1, keepdims=True) + NORM_EPS)
    o_ref[...] = (y * g_ref[...]).astype(o_ref.dtype)


def _rmsnorm(x, gain, out_dtype):
    m, d = x.shape
    tm = min(NORM_ROWS, m)
    return pl.pallas_call(
        _rmsnorm_kernel,
        grid=(m // tm,),
        in_specs=[pl.BlockSpec((tm, d), lambda i: (i, 0)),
                  pl.BlockSpec((1, d), lambda i: (0, 0))],
        out_specs=pl.BlockSpec((tm, d), lambda i: (i, 0)),
        out_shape=jax.ShapeDtypeStruct((m, d), out_dtype),
        compiler_params=_params(("arbitrary",)),
        name="rmsnorm",
    )(x, gain.reshape(1, d).astype(F32))


def _ws_matmul_kernel(*refs, n_w, n_rows, n_ex_p, n_ex_s, n_out_p, n_out_s, tn, body_p, body_s,
                      folded_norm):
    it = iter(refs)
    take = lambda n: [next(it) for _ in range(n)]
    (a_p, a_s), w_refs, row_refs = take(2), take(n_w), take(n_rows)
    ssq_p, ssq_s = take(2) if folded_norm else (None, None)
    ex_p, ex_s, outs_p, outs_s = take(n_ex_p), take(n_ex_s), take(n_out_p), take(n_out_s)
    wcat, *scratch = list(it)
    first = pl.program_id(1) == 0

    def products(a_ref, ssq_ref):
        def zs(rs):
            z = jnp.dot(a_ref[rs, :], wcat[...], preferred_element_type=F32)
            if folded_norm:
                mean_sq = jnp.sum(ssq_ref[rs, :], axis=-1, keepdims=True) * (1.0 / a_ref.shape[1])
                z = z * lax.rsqrt(mean_sq + NORM_EPS)
            return [z[:, c * tn:(c + 1) * tn] for c in range(n_w)]
        return zs

    @pl.when(first)
    def _():
        for c, w_ref in enumerate(w_refs):
            wcat[:, c * tn:(c + 1) * tn] = w_ref[...].astype(BF16)
        body_s(products(a_s, ssq_s), a_s.shape[0], row_refs, ex_s, outs_s, scratch, None)

    body_p(products(a_p, ssq_p), a_p.shape[0], row_refs, ex_p, outs_p, scratch, first)


def _ws_matmul(a_p, a_s, w, layer, col_offsets, tn, rows, extras_p, extras_s, outs_p, outs_s,
               body_p, body_s, tm, name, scratch=(), norm=None, weight_buffers=2):
    m_p, k = a_p.shape
    m_s = a_s.shape[0]
    tm = min(tm, m_p)
    n_w = len(col_offsets)
    assert m_p % tm == 0 and all(c % tn == 0 for c in col_offsets)
    nj = rows[0].shape[1] // tn if rows else outs_p[0][0][-1] // tn
    once = pl.Buffered(1)
    in_specs = [pl.BlockSpec((tm, k), lambda j, i: (i, 0)),
                pl.BlockSpec((m_s, k), lambda j, i: (0, 0), pipeline_mode=once)]
    for c in col_offsets:
        in_specs.append(pl.BlockSpec((None, k, tn), lambda j, i, c=c: (layer, 0, c // tn + j),
                                     pipeline_mode=pl.Buffered(weight_buffers)))
    for r in rows:
        in_specs.append(pl.BlockSpec((r.shape[0], tn), lambda j, i: (0, j)))
    norm_args = []
    if norm is not None:
        ssq_p, ssq_s = norm_args = list(norm)
        in_specs += [pl.BlockSpec((tm, ssq_p.shape[1]), lambda j, i: (i, 0)),
                     pl.BlockSpec((m_s, ssq_s.shape[1]), lambda j, i: (0, 0), pipeline_mode=once)]
    in_specs += [pl.BlockSpec(blk, imap) for _, blk, imap in list(extras_p) + list(extras_s)]
    outs = list(outs_p) + list(outs_s)
    kern = functools.partial(
        _ws_matmul_kernel, n_w=n_w, n_rows=len(rows), n_ex_p=len(extras_p), n_ex_s=len(extras_s),
        folded_norm=norm is not None,
        n_out_p=len(outs_p), n_out_s=len(outs_s), tn=tn, body_p=body_p, body_s=body_s)
    res = pl.pallas_call(
        kern,
        grid=(nj, m_p // tm),
        in_specs=in_specs,
        out_specs=[pl.BlockSpec(blk, imap) for _, _, blk, imap in outs],
        out_shape=[jax.ShapeDtypeStruct(shape, dt) for shape, dt, _, _ in outs],
        scratch_shapes=[pltpu.VMEM((k, n_w * tn), BF16)] + list(scratch),
        compiler_params=_params(("arbitrary", "arbitrary")),
        name=name,
    )(a_p, a_s, *([w] * n_w), *rows, *norm_args, *[x for x, _, _ in extras_p],
      *[x for x, _, _ in extras_s])
    return res[:len(outs_p)], res[len(outs_p):]


def _rowwise_body(epilogue):
    def body(zs, m, row_refs, extra_refs, out_refs, scratch, first):
        sub = min(ROW_SUBBLOCK, m)
        starts = list(range(0, m, sub))
        sizes = [sub] * len(starts)
        if len(starts) > 1 and sub % (2 * HEAD_DIM) == 0:
            starts, sizes = starts + [starts[-1] + sub // 2], sizes[:-1] + [sub // 2] * 2
        row_vals = [r[...] for r in row_refs]
        for r0, size in zip(starts, sizes):
            for o_ref, r in zip(out_refs, epilogue(zs(slice(r0, r0 + size)), row_vals)):
                n = r.shape[0]
                lo = r0 * n // size
                o_ref[lo:lo + n, :] = r.astype(o_ref.dtype)
    return body


def _row_major(m, tm, n_cols, tn, dtype, sample):
    imap = (lambda j, i: (0, j)) if sample else (lambda j, i: (i, j))
    return ((m, n_cols), dtype, (tm, tn), imap)


def _head_major(m, tm, n_heads, tn, dtype, sample, group=1):
    imap = (lambda j, i: (j, 0, 0)) if sample else (lambda j, i: (j, i, 0))
    return ((n_heads, m // group, tn), dtype, (None, tm // group, tn), imap)


def _silu(z):
    return z * jax.nn.sigmoid(z)


def _hgrn_proj_epilogue(zs, row_vals, *, layer, chunk):
    zq, zf, zi, zg = zs
    logits = row_vals[0]
    e = jnp.exp(logits - jnp.max(logits, axis=0, keepdims=True))
    lb = jnp.sum(e[:layer + 1], axis=0, keepdims=True) / jnp.sum(e, axis=0, keepdims=True)
    f = lb + (1.0 - lb) * jax.nn.sigmoid(zf)
    k = 1.0 - f
    g = jnp.log(f)
    chunk_sums = jnp.sum(g.reshape(g.shape[0] // chunk, chunk, g.shape[1]), axis=1)
    return [_silu(zq), k, g, zi, _silu(zg), chunk_sums]


def _swiglu_epilogue(zs, row_vals):
    return [_silu(zs[0]) * zs[1]]


def _conv_rows(cu, prev2, prev1, w):
    row = lax.broadcasted_iota(jnp.int32, cu.shape, 0)
    c1 = jnp.where(row == 0, prev1, pltpu.roll(cu, 1, 0))
    c2 = jnp.where(row == 0, prev2, jnp.where(row == 1, prev1, pltpu.roll(cu, 2, 0)))
    return w[0:1, :] * c2 + w[1:2, :] * c1 + w[2:3, :] * cu


def _conv_body_prompt(zs, m, row_refs, extra_refs, out_refs, scratch, first):
    (buf_ref,), (gated_ref, state_ref), (carry_ref,) = extra_refs, out_refs, scratch
    w = row_refs[0][...]

    @pl.when(first)
    def _():
        carry_ref[...] = buf_ref[...]

    prev2, prev1 = carry_ref[0:1, :], carry_ref[1:2, :]
    sub = min(ROW_SUBBLOCK, m)
    for r0 in range(0, m, sub):
        gb, gc, u = zs(slice(r0, r0 + sub))
        cu = gc * u
        gated_ref[r0:r0 + sub, :] = (gb * _conv_rows(cu, prev2, prev1, w)).astype(gated_ref.dtype)
        prev2, prev1 = cu[sub - 2:sub - 1, :], cu[sub - 1:sub, :]
    last = jnp.concatenate([prev2, prev1], axis=0)
    carry_ref[...] = last
    state_ref[...] = last


def _conv_body_sample(zs, m, row_refs, extra_refs, out_refs, scratch, first, *, frames):
    (buf_ref,), (gated_ref, state_ref) = extra_refs, out_refs
    w = row_refs[0][...]
    gb, gc, u = zs(slice(0, m))
    cu = gc * u
    for b in range(m // frames):
        rs = slice(b * frames, (b + 1) * frames)
        cub = cu[rs, :]
        buf = buf_ref[b]
        conv = _conv_rows(cub, buf[0:1, :], buf[1:2, :], w)
        gated_ref[rs, :] = (gb[rs, :] * conv).astype(gated_ref.dtype)
        state_ref[b] = cub[frames - 2:frames, :]


def _matmul_kernel(a_ref, w_ref, x_ref, *refs):
    m = a_ref.shape[0]
    sub = min(ROW_SUBBLOCK, m)
    parts = []
    for r0 in range(0, m, sub):
        rs = slice(r0, r0 + sub)
        x = x_ref[rs, :] + jnp.dot(a_ref[rs, :], w_ref[...], preferred_element_type=F32)
        if len(refs) == 1:
            (o_ref,) = refs
        else:
            gain_ref, o_ref, xb_ref, ssq_ref = refs
            xb_ref[rs, :] = (x * gain_ref[...]).astype(BF16)
            sq = x * x
            lanes = ssq_ref.shape[1]
            part = sq[:, 0:lanes]
            for c in range(lanes, sq.shape[1], lanes):
                part = part + sq[:, c:c + lanes]
            parts.append(part)
        o_ref[rs, :] = x
    if parts:
        total = jnp.concatenate(parts, axis=0)

        @pl.when(pl.program_id(1) == 0)
        def _():
            ssq_ref[...] = total

        @pl.when(pl.program_id(1) > 0)
        def _():
            ssq_ref[...] += total


def _matmul_residual(a, w, layer, x, tm, tn, k_splits, name, norm_gain=None):
    m, k = a.shape
    n = w.shape[2]
    tm = min(tm, m)
    tn = min(tn, n)
    tk = k // k_splits
    assert m % tm == 0 and n % tn == 0 and k % k_splits == 0
    lanes = min(HEAD_DIM, tn)
    for ks in range(k_splits):
        tile = pl.BlockSpec((tm, tn), lambda i, j: (i, j))
        out_specs, out_shape = [tile], [jax.ShapeDtypeStruct((m, n), F32)]
        in_specs = [pl.BlockSpec((tm, tk), lambda i, j, ks=ks: (i, ks)),
                    pl.BlockSpec((None, tk, tn), lambda i, j, ks=ks: (layer, ks, j)), tile]
        args = [a, w, x]
        if norm_gain is not None and ks == k_splits - 1:
            in_specs.append(pl.BlockSpec((1, tn), lambda i, j: (0, j)))
            args.append(norm_gain.reshape(1, n).astype(F32))
            out_specs += [tile, pl.BlockSpec((tm, lanes), lambda i, j: (i, 0))]
            out_shape += [jax.ShapeDtypeStruct((m, n), BF16), jax.ShapeDtypeStruct((m, lanes), F32)]
        res = pl.pallas_call(
            _matmul_kernel,
            grid=(m // tm, n // tn),
            in_specs=in_specs,
            out_specs=out_specs,
            out_shape=out_shape,
            compiler_params=_params(("arbitrary", "arbitrary")),
            name=name,
        )(*args)
        x = res[0]
    return x if norm_gain is None else tuple(res)


def _neg_abs(x):
    bits = lax.bitcast_convert_type(x, jnp.uint32) | jnp.uint32(0x80000000)
    return lax.bitcast_convert_type(bits, F32)


class _ChunkSums:
    def __init__(self, g, p_ref, chunk):
        self.chunk, self.n, self.p_ref = chunk, chunk // SUBLANES, p_ref
        n = self.n
        self.r8 = lax.broadcasted_iota(jnp.int32, (SUBLANES, HEAD_DIM), 0)
        self.tiles = [g[SUBLANES * j:SUBLANES * (j + 1)] for j in range(n)]
        self.pre = []
        for t in self.tiles:
            p = t
            d = 1
            while d < SUBLANES:
                p = p + jnp.where(self.r8 >= d, pltpu.roll(p, d, 0), 0.0)
                d *= 2
            self.pre.append(p)
        p_ref[...] = jnp.concatenate(self.pre, axis=0)
        tot = [self.row(SUBLANES * j + SUBLANES - 1) for j in range(n)]
        self.suf = [tot[j] - self.pre[j] for j in range(n)]
        self.end = [tot[0]]
        for j in range(1, n):
            self.end.append(self.end[j - 1] + tot[j])

    def row(self, i):
        return jnp.broadcast_to(self.p_ref[i:i + 1, :], (SUBLANES, HEAD_DIM))

    def upper(self, j, jm):
        return self.pre[j] if j - 1 == jm else self.pre[j] + (self.end[j - 1] - self.end[jm])

    def lower(self, j, jm):
        return self.suf[j] if j == jm else self.suf[j] + (self.end[jm] - self.end[j])

    def from_start_and_to_end(self):
        p_c = [self.pre[j] + self.end[j - 1] if j else self.pre[0] for j in range(self.n)]
        x_c = [self.end[-1] - p for p in p_c]
        return jnp.concatenate(p_c, axis=0), jnp.concatenate(x_c, axis=0)

    def levels(self):
        r8, n = self.r8, self.n
        out = {1: jnp.concatenate([jnp.where((r8 & 1) == 1, t, 0.0) for t in self.tiles], axis=0)}
        m = 2
        while m < SUBLANES:
            level = []
            for j in range(n):
                base = SUBLANES * j
                mid = self.row(base + m - 1)
                for blk in range(1, SUBLANES // (2 * m)):
                    mid = jnp.where(r8 < 2 * m * blk, mid, self.row(base + 2 * m * blk + m - 1))
                level.append(_neg_abs(self.pre[j] - mid))
            out[m] = jnp.concatenate(level, axis=0)
            m *= 2
        while m < self.chunk:
            per = 2 * m // SUBLANES
            level = []
            for j in range(n):
                jm = (j // per) * per + per // 2 - 1
                level.append(self.upper(j, jm) if j > jm else self.lower(j, jm))
            out[m] = jnp.concatenate(level, axis=0)
            m *= 2
        return out


def _hgrn_kernel(direct_ref, q_ref, k_ref, g_ref, v_ref, z_ref, s0_ref, gain_ref, *refs, chunk,
                 n_chunks, heads, n_casts):
    cast_in, (o_ref, sfin_ref), cast_out = refs[:n_casts], refs[n_casts:n_casts + 2], \
        refs[n_casts + 2:2 * n_casts + 2]
    st_ref, p_ref = refs[2 * n_casts + 2:]
    for src_ref, dst_ref in zip(cast_in, cast_out):
        dst_ref[...] = src_ref[...].astype(dst_ref.dtype)
    bi, hi, li = pl.program_id(0), pl.program_id(1), pl.program_id(2)
    direct = direct_ref[(bi * pl.num_programs(2) + li) * pl.num_programs(1) + hi] == 1

    @pl.when(li == 0)
    def _():
        for hb in range(heads):
            st_ref[hb] = s0_ref[0, hb].T

    t_idx = lax.broadcasted_iota(jnp.int32, (chunk, chunk), 0)
    s_idx = lax.broadcasted_iota(jnp.int32, (chunk, chunk), 1)
    differ = t_idx ^ s_idx
    masks = []
    m = 1
    while m < chunk:
        masks.append((m, (t_idx > s_idx) & (differ >= m) & (differ < 2 * m)))
        m *= 2
    gain = gain_ref[...]
    nt_dims = (((1,), (1,)), ((), ()))
    tn_dims = (((0,), (0,)), ((), ()))

    def pair_rows(x0, x1):
        zero = jnp.zeros_like(x0)
        return jnp.concatenate([jnp.concatenate([x0, zero], axis=1),
                                jnp.concatenate([zero, x1], axis=1)], axis=0)

    def paired_nt(lhs, rhs):
        out = []
        for i in range(0, len(lhs), 2):
            if i + 1 < len(lhs):
                r = lax.dot_general(pair_rows(lhs[i], lhs[i + 1]),
                                    jnp.concatenate([rhs[i], rhs[i + 1]], axis=1), nt_dims,
                                    preferred_element_type=F32)
                half = lhs[i].shape[0]
                out += [r[:half], r[half:]]
            else:
                out.append(lax.dot_general(lhs[i], rhs[i], nt_dims, preferred_element_type=F32))
        return out

    def finish(rows, a_bf, q_in, k_out, decay):
        sts = [st_ref[hb] for hb in range(heads)]
        o_inter = paired_nt(q_in, [st.astype(BF16) for st in sts])
        for hb in range(heads):
            lanes = slice(hb * HEAD_DIM, (hb + 1) * HEAD_DIM)
            v = v_ref[hb, rows, :]
            o = o_inter[hb] + jnp.dot(a_bf[hb], v, preferred_element_type=F32)
            st_ref[hb] = decay[hb] * sts[hb] + lax.dot_general(v, k_out[hb], tn_dims,
                                                               preferred_element_type=F32)
            o = o * lax.rsqrt(jnp.mean(o * o, axis=-1, keepdims=True) + NORM_EPS) * gain
            o_ref[rows, lanes] = (o * z_ref[hb, rows, :]).astype(o_ref.dtype)

    def chunk_rows(c):
        return pl.ds(pl.multiple_of(c * chunk, chunk), chunk)

    def level_body(c, carry):
        rows = chunk_rows(c)
        a_bf, q_in, k_out, decay = [], [], [], []
        for hb in range(heads):
            q = q_ref[hb, rows, :]
            k = k_ref[hb, rows, :]
            sums = _ChunkSums(g_ref[hb, rows, :], p_ref.at[hb], chunk)
            levels, (p_c, x_c) = sums.levels(), sums.from_start_and_to_end()
            a = jnp.where(t_idx == s_idx, jnp.sum(q * k, axis=-1, keepdims=True), 0.0)
            es = [jnp.exp(levels[m]) for m, _ in masks]
            parts = paired_nt([(q * e).astype(BF16) for e in es], [(k * e).astype(BF16) for e in es])
            for (_, mask), part in zip(masks, parts):
                a = jnp.where(mask, part, a)
            a_bf.append(a.astype(BF16))
            q_in.append((q * jnp.exp(p_c)).astype(BF16))
            k_out.append((k * jnp.exp(x_c)).astype(BF16))
            decay.append(jnp.exp(p_c[chunk - 1:chunk, :]))
        finish(rows, a_bf, q_in, k_out, decay)
        return carry

    def direct_body(c, carry):
        rows = chunk_rows(c)
        q_in, k_neg, k_out, decay = [], [], [], []
        for hb in range(heads):
            q = q_ref[hb, rows, :]
            k = k_ref[hb, rows, :]
            sums = _ChunkSums(g_ref[hb, rows, :], p_ref.at[hb], chunk)
            p_c, x_c = sums.from_start_and_to_end()
            q_in.append((q * jnp.exp(p_c)).astype(BF16))
            k_neg.append((k * jnp.exp(-p_c)).astype(BF16))
            k_out.append((k * jnp.exp(x_c)).astype(BF16))
            decay.append(jnp.exp(p_c[chunk - 1:chunk, :]))
        a_bf = [jnp.where(t_idx >= s_idx, a, 0.0).astype(BF16) for a in paired_nt(q_in, k_neg)]
        finish(rows, a_bf, q_in, k_out, decay)
        return carry

    @pl.when(direct)
    def _():
        lax.fori_loop(0, n_chunks, direct_body, 0, unroll=min(8, n_chunks))

    @pl.when(jnp.logical_not(direct))
    def _():
        lax.fori_loop(0, n_chunks, level_body, 0)

    @pl.when(li == pl.num_programs(2) - 1)
    def _():
        for hb in range(heads):
            sfin_ref[0, hb] = st_ref[hb].T


def _cast_blocks(shape, n_steps):
    n_rows, n_cols = shape
    col_blocks = 1
    while col_blocks <= n_steps:
        row_blocks = n_steps // col_blocks
        if (row_blocks * col_blocks == n_steps and n_rows % row_blocks == 0
                and n_cols % col_blocks == 0 and (n_rows // row_blocks) % (2 * SUBLANES) == 0
                and (n_cols // col_blocks) % HEAD_DIM == 0):
            return n_rows // row_blocks, n_cols // col_blocks
        col_blocks *= 2
    return None


def _hgrn_recurrence(q, k, g, v, z, chunk_sums, s0, gain, n_streams, chunk, block_frames,
                     cast_weights=()):
    h, m, _ = q.shape
    l = m // n_streams
    t = min(block_frames, l)
    hb = min(HEADS_PER_STEP * (2 if l == chunk else 1), h)
    assert l % t == 0 and t % chunk == 0 and h % hb == 0
    nl = l // t
    least = jnp.min(chunk_sums.reshape(h // hb, hb, n_streams * nl, t // chunk, HEAD_DIM),
                    axis=(1, 3, 4))
    direct = (least.T >= DIRECT_DECAY_MIN_LOG).astype(jnp.int32).reshape(-1)
    seq = pl.BlockSpec((hb, t, HEAD_DIM), lambda bi, hi, li, flags: (hi, bi * nl + li, 0))
    state = pl.BlockSpec((1, hb, HEAD_DIM, HEAD_DIM), lambda bi, hi, li, flags: (bi, hi, 0, 0))
    n_groups = h // hb
    n_steps = n_streams * n_groups * nl
    flat = [w.reshape(-1, w.shape[-1]) for w in cast_weights]
    blocks = [_cast_blocks(w.shape, n_steps) for w in flat]
    assert all(b is not None for b in blocks)
    cast_specs = [pl.BlockSpec(b, lambda bi, hi, li, flags, cb=w.shape[1] // b[1]: (
        ((bi * n_groups + hi) * nl + li) // cb, ((bi * n_groups + hi) * nl + li) % cb))
        for w, b in zip(flat, blocks)]
    kern = functools.partial(_hgrn_kernel, chunk=chunk, n_chunks=t // chunk, heads=hb,
                             n_casts=len(flat))
    res = pl.pallas_call(
        kern,
        grid_spec=pltpu.PrefetchScalarGridSpec(
            num_scalar_prefetch=1,
            grid=(n_streams, n_groups, nl),
            in_specs=[seq, seq, seq, seq, seq, state,
                      pl.BlockSpec((1, HEAD_DIM), lambda bi, hi, li, flags: (0, 0))] + cast_specs,
            out_specs=[pl.BlockSpec((t, hb * HEAD_DIM),
                                    lambda bi, hi, li, flags: (bi * nl + li, hi)), state] + cast_specs,
            scratch_shapes=[pltpu.VMEM((hb, HEAD_DIM, HEAD_DIM), F32),
                            pltpu.VMEM((hb, chunk, HEAD_DIM), F32)]),
        out_shape=[jax.ShapeDtypeStruct((m, h * HEAD_DIM), BF16),
                   jax.ShapeDtypeStruct(s0.shape, s0.dtype)]
        + [jax.ShapeDtypeStruct(w.shape, BF16) for w in flat],
        compiler_params=_params(("arbitrary", "arbitrary", "arbitrary")),
        name="hgrn_recurrence",
    )(direct, q, k, g, v, z, s0, gain.reshape(1, HEAD_DIM).astype(F32), *flat)
    return res[0], res[1], [wb.reshape(w.shape) for wb, w in zip(res[2:], cast_weights)]


def _ffn(xs_p, xs_s, w_in, w_out_bf16, layer, d_ff, next_gain):
    (x_p, xb_p, ssq_p), (x_s, xb_s, ssq_s) = xs_p, xs_s
    m_p, m_s = x_p.shape[0], x_s.shape[0]
    wide = layer == 0 and m_p % (2 * WS_ROWS) == 0
    tm, tn = min(WS_ROWS * (2 if wide else 1), m_p), WS_COLS
    body = _rowwise_body(_swiglu_epilogue)
    (act_p,), (act_s,) = _ws_matmul(
        xb_p, xb_s, w_in, layer, (0, d_ff), tn, [], [], [],
        [_row_major(m_p, tm, d_ff, tn, BF16, False)], [_row_major(m_s, m_s, d_ff, tn, BF16, True)],
        body, body, tm, "ffn_in", norm=(ssq_p, ssq_s), weight_buffers=1 if wide else 2)
    out_p, out_s = (_matmul_residual(act, w_out_bf16, layer, x, OUT_ROWS, OUT_COLS, FFN_OUT_K_SPLITS,
                                     "ffn_out", next_gain) for act, x in ((act_p, x_p), (act_s, x_s)))
    return out_p, out_s


def _mixer_out(a_p, a_s, w_bf16, x_p, x_s, next_gain, name):
    out_p, out_s = (_matmul_residual(a, w_bf16, 0, x, OUT_ROWS, OUT_COLS, 1, name, next_gain)
                    for a, x in ((a_p, x_p), (a_s, x_s)))
    return out_p, out_s


def kernel(x_prompt, x_sample, state_hgrn, state_conv, hgrn_lb_logits, hgrn_w_in, hgrn_out_gain,
           hgrn_w_out, conv_w_in, conv_w, conv_w_out, norm_mix, norm_ffn, ffn_w_in, ffn_w_out,
           norm_final):
    bp, lp, d = x_prompt.shape
    bs, ls, _ = x_sample.shape
    assert bp == 1
    m_p, m_s = bp * lp, bs * ls
    d_ff = ffn_w_out.shape[1]
    n_heads = d // HEAD_DIM
    x_p = x_prompt.reshape(m_p, d)
    x_s = x_sample.reshape(m_s, d)

    h_p = _rmsnorm(x_p, norm_mix[0], BF16)
    h_s = _rmsnorm(x_s, norm_mix[0], BF16)
    tm = min(WS_ROWS, m_p)
    proj_dtypes = [F32, F32, F32, BF16, F32]
    head_outs = lambda m, t, chunk, sample: (
        [_head_major(m, t, n_heads, HEAD_DIM, dt, sample) for dt in proj_dtypes]
        + [_head_major(m, t, n_heads, HEAD_DIM, F32, sample, group=chunk)])
    epilogue = lambda chunk: _rowwise_body(
        functools.partial(_hgrn_proj_epilogue, layer=0, chunk=chunk))
    proj_p, proj_s = _ws_matmul(
        h_p, h_s, hgrn_w_in, 0, (0, d, 2 * d, 3 * d), HEAD_DIM, [hgrn_lb_logits.astype(F32)], [], [],
        head_outs(m_p, tm, PROMPT_CHUNK, False), head_outs(m_s, m_s, ls, True),
        epilogue(PROMPT_CHUNK), epilogue(ls), tm, "hgrn_proj")
    zero_state = jnp.zeros((bp,) + state_hgrn.shape[2:], state_hgrn.dtype)
    o_p, hgrn_p, (hgrn_w_out_bf16, conv_w_out_bf16, ffn_w_out_bf16) = _hgrn_recurrence(
        *proj_p, zero_state, hgrn_out_gain[0], bp, PROMPT_CHUNK, block_frames=RECURRENCE_FRAMES,
        cast_weights=(hgrn_w_out, conv_w_out, ffn_w_out))
    o_s, hgrn_s, _ = _hgrn_recurrence(*proj_s, state_hgrn[0], hgrn_out_gain[0], bs, ls,
                                      block_frames=ls)
    xs_p, xs_s = _mixer_out(o_p, o_s, hgrn_w_out_bf16, x_p, x_s, norm_ffn[0], "hgrn_out")
    xs_p, xs_s = _ffn(xs_p, xs_s, ffn_w_in, ffn_w_out_bf16, 0, d_ff, next_gain=norm_mix[1])

    (x_p, xb_p, ssq_p), (x_s, xb_s, ssq_s) = xs_p, xs_s
    tm, tn = min(WS_ROWS, m_p), WS_COLS
    two_rows = lambda b: ((b, 2, tn), lambda j, i: (0, 0, j))
    zero_buf = jnp.zeros((bp,) + state_conv.shape[2:], state_conv.dtype)
    (gated_p, conv_p), (gated_s, conv_s) = _ws_matmul(
        xb_p, xb_s, conv_w_in, 0, (0, d, 2 * d), tn, [conv_w[0].astype(F32)],
        [(zero_buf, (None, 2, tn), two_rows(None)[1])], [(state_conv[0],) + two_rows(bs)],
        [_row_major(m_p, tm, d, tn, BF16, False),
         ((bp, 2, d), state_conv.dtype, (None, 2, tn), two_rows(None)[1])],
        [_row_major(m_s, m_s, d, tn, BF16, True), ((bs, 2, d), state_conv.dtype) + two_rows(bs)],
        _conv_body_prompt, functools.partial(_conv_body_sample, frames=ls), tm, "conv_in",
        scratch=[pltpu.VMEM((2, tn), F32)], norm=(ssq_p, ssq_s))
    xs_p, xs_s = _mixer_out(gated_p, gated_s, conv_w_out_bf16, x_p, x_s, norm_ffn[1], "conv_out")
    x_p, x_s = _ffn(xs_p, xs_s, ffn_w_in, ffn_w_out_bf16, 1, d_ff, next_gain=None)

    y_p = _rmsnorm(x_p, norm_final, F32).reshape(bp, lp, d)
    y_s = _rmsnorm(x_s, norm_final, F32).reshape(bs, ls, d)
    return (y_p, y_s, hgrn_p[None], hgrn_s[None], conv_p[None], conv_s[None])
```
